```python
import math
import jax, jax.numpy as jnp
from jax import lax
import numpy as np

D_MODEL = 4096
BATCH = 2
SEQ = 8192
DEPTH = 2

MIX_WIDTH = D_MODEL
RET_HEAD_DIM = 256
RET_WIDTH = D_MODEL // 2
RET_HEADS = RET_WIDTH // RET_HEAD_DIM
RET_CHUNK = 128
HYENA_WIDTH = D_MODEL // 4
HYENA_ORDER = 2
HYENA_EMB_DIM = 33
HYENA_FILTER_HIDDEN = 64
HYENA_SHORT_CONV = 3
HYENA_DECAY_TARGET = 1e-2
HYENA_FAST_DECAY_PCT = 0.3
HYENA_SLOW_DECAY_PCT = 1.5
LRU_WIDTH = D_MODEL // 4
LRU_BLOCK_DIM = 128
LRU_BLOCKS = LRU_WIDTH // LRU_BLOCK_DIM
LRU_CONV = 4
LRU_C = 8.0
D_FF = 256 * ((8 * D_MODEL // 3 + 255) // 256)
IN_WIDTH = 4 * RET_WIDTH + 3 * HYENA_WIDTH + 2 * LRU_WIDTH
DEEPNORM_ALPHA = (2 * DEPTH) ** 0.25
DEEPNORM_BETA = (8 * DEPTH) ** -0.25

kernel_name = "hybrid_retention_hyena_rglru_encoder"


def layer_norm(x, g, b, eps=1e-5):
    xf = x.astype(jnp.float32)
    mu = jnp.mean(xf, axis=-1, keepdims=True)
    var = jnp.mean(jnp.square(xf - mu), axis=-1, keepdims=True)
    y = (xf - mu) * lax.rsqrt(var + eps) * g.astype(jnp.float32) + b.astype(jnp.float32)
    return y.astype(x.dtype)


def rms_norm(x, g, eps=1e-6):
    xf = x.astype(jnp.float32)
    y = xf * lax.rsqrt(jnp.mean(jnp.square(xf), axis=-1, keepdims=True) + eps) * g.astype(jnp.float32)
    return y


def swiglu(x, w_gate, w_up, w_down):
    return (jax.nn.silu(x @ w_gate) * (x @ w_up)) @ w_down


def depthwise_conv_centred(x, w, b):
    K = w.shape[0]
    y = lax.conv_general_dilated(
        x, w[:, None, :].astype(x.dtype), window_strides=(1,),
        padding=[((K - 1) // 2, K // 2)],
        dimension_numbers=('NWC', 'WIO', 'NWC'),
        feature_group_count=x.shape[-1])
    return y + b.astype(x.dtype)


def rotary(x, pos):
    half = x.shape[-1] // 2
    inv_freq = 1.0 / (10000.0 ** jnp.linspace(0.0, 1.0, half, dtype=jnp.float32))
    ang = pos.astype(jnp.float32)[:, None] * inv_freq[None, :]
    cos = jnp.cos(ang)[None, :, None, :]
    sin = jnp.sin(ang)[None, :, None, :]
    x1, x2 = x[..., :half], x[..., half:]
    return jnp.concatenate([x1 * cos - x2 * sin, x1 * sin + x2 * cos], axis=-1)


def retention_bidirectional(q, k, v):
    B_, S_, H, Dk = q.shape
    Dv = v.shape[-1]
    C = RET_CHUNK
    nc = S_ // C

    def chunks(t):
        return t.reshape(B_, nc, C, H, t.shape[-1]).transpose(1, 0, 3, 2, 4)

    qc, kc, vc = chunks(q), chunks(k), chunks(v)
    log_g = jnp.log1p(-jnp.exp2(-5.0 - jnp.arange(H, dtype=jnp.float32)))
    idx = jnp.arange(C, dtype=jnp.float32)
    intra = jnp.exp(log_g[:, None, None] * jnp.abs(idx[:, None] - idx[None, :]))
    q_fwd = jnp.exp(log_g[:, None] * (idx + 1.0))[None, :, :, None]
    k_fwd = jnp.exp(log_g[:, None] * (C - 1.0 - idx))[None, :, :, None]
    q_bwd = jnp.exp(log_g[:, None] * (C - idx))[None, :, :, None]
    k_bwd = jnp.exp(log_g[:, None] * idx)[None, :, :, None]
    chunk_decay = jnp.exp(log_g * C)[None, :, None, None]

    def fwd_step(state, inp):
        qi, ki, vi = inp
        s = jnp.einsum('bhid,bhjd->bhij', qi, ki) * intra
        y = jnp.einsum('bhij,bhje->bhie', s, vi) + jnp.einsum('bhid,bhde->bhie', qi * q_fwd, state)
        state = state * chunk_decay + jnp.einsum('bhjd,bhje->bhde', ki * k_fwd, vi)
        return state, y

    def bwd_step(state, inp):
        qi, ki, vi = inp
        y = jnp.einsum('bhid,bhde->bhie', qi * q_bwd, state)
        state = state * chunk_decay + jnp.einsum('bhjd,bhje->bhde', ki * k_bwd, vi)
        return state, y

    s0 = jnp.zeros((B_, H, Dk, Dv), jnp.float32)
    _, y_f = lax.scan(fwd_step, s0, (qc, kc, vc))
    _, y_b = lax.scan(bwd_step, s0, (qc, kc, vc), reverse=True)
    return (y_f + y_b).transpose(1, 0, 3, 2, 4).reshape(B_, S_, H, Dv)


def hyena_filters(L, w1, b1, freq1, w2, b2, freq2, w3, decay):
    f32 = jnp.float32
    pos = jnp.arange(L, dtype=f32)
    t = pos / max(L - 1, 1)
    bands = (HYENA_EMB_DIM - 1) // 2
    fr = jnp.linspace(1e-4, bands - 1, bands, dtype=f32)
    w = 2.0 * math.pi * pos / L
    z = jnp.concatenate([t[:, None], jnp.cos(w[:, None] * fr), -jnp.sin(w[:, None] * fr)], axis=-1)
    h = jnp.sin(freq1.astype(f32) * (z @ w1.astype(f32) + b1.astype(f32)))
    h = jnp.sin(freq2.astype(f32) * (h @ w2.astype(f32) + b2.astype(f32)))
    h = h @ w3.astype(f32)
    h = h * jnp.exp(-t[:, None] * jnp.abs(decay.astype(f32))[None, :])
    return h.reshape(L, 2, HYENA_ORDER, HYENA_WIDTH)


def hyena_bidirectional(u, conv_w, conv_b, filters, bias):
    u = depthwise_conv_centred(u, conv_w, conv_b)
    v, x1, x2 = jnp.split(u, 3, axis=-1)
    L = u.shape[1]
    h_fwd, h_bwd = filters[:, 0], filters[:, 1]
    filt2 = jnp.concatenate([h_fwd, jnp.zeros_like(h_fwd[:1]), jnp.flip(h_bwd[1:], axis=0)], axis=0)
    filt2 = filt2 / (jnp.sum(jnp.abs(filt2), axis=0, keepdims=True) + 1e-6)
    filt_f = jnp.fft.rfft(filt2, axis=0)
    z = v.astype(jnp.float32)
    for n, gate in enumerate((x1, x2)):
        zf = jnp.fft.rfft(z, n=2 * L, axis=1)
        conv = jnp.fft.irfft(zf * filt_f[None, :, n], n=2 * L, axis=1)[:, :L]
        z = gate.astype(jnp.float32) * (conv + bias[n].astype(jnp.float32) * z)
    return z


def _linear_combine(e1, e2):
    a1, b1 = e1
    a2, b2 = e2
    return a1 * a2, a2 * b1 + b2


def rglru_bidirectional(gate_in, x_in, conv_w, conv_b, wa, ba, wx, bx, lam):
    f32 = jnp.float32
    gate = jax.nn.gelu(gate_in.astype(f32))
    xr = depthwise_conv_centred(x_in, conv_w, conv_b).astype(f32)
    B_, S_, _ = xr.shape
    xb = xr.reshape(B_, S_, LRU_BLOCKS, LRU_BLOCK_DIM)

    def direction(d, rev):
        r = jax.nn.sigmoid(jnp.einsum('bsni,nij->bsnj', xb, wa[d].astype(f32)).reshape(B_, S_, LRU_WIDTH) + ba[d].astype(f32))
        i = jax.nn.sigmoid(jnp.einsum('bsni,nij->bsnj', xb, wx[d].astype(f32)).reshape(B_, S_, LRU_WIDTH) + bx[d].astype(f32))
        log_a = -LRU_C * r * jax.nn.softplus(-lam[d].astype(f32))
        a = jnp.exp(log_a)
        b = jnp.sqrt(-jnp.expm1(2.0 * log_a)) * (i * xr)
        _, h = lax.associative_scan(_linear_combine, (a, b), reverse=rev, axis=1)
        return h

    h = direction(0, False) + direction(1, True)
    return gate * h


def hybrid_mixer(x, pos, w_in, w_out, mix_gain, hy_conv_w, hy_conv_b, hy_filters, hy_bias,
                 lru_conv_w, lru_conv_b, lru_wa, lru_ba, lru_wx, lru_bx, lru_lambda):
    B_, S_, _ = x.shape
    p = x @ w_in
    R, Hy, Lw = RET_WIDTH, HYENA_WIDTH, LRU_WIDTH
    cuts = [R, 2 * R, 3 * R, 4 * R, 4 * R + 3 * Hy, 4 * R + 3 * Hy + Lw]
    q, k, v, g, hy_u, lru_g, lru_x = jnp.split(p, cuts, axis=-1)
    f32 = jnp.float32
    q = rotary(q.astype(f32).reshape(B_, S_, RET_HEADS, RET_HEAD_DIM), pos)
    k = rotary(k.astype(f32).reshape(B_, S_, RET_HEADS, RET_HEAD_DIM), pos) * (RET_HEAD_DIM ** -0.5)
    v = v.astype(f32).reshape(B_, S_, RET_HEADS, RET_HEAD_DIM)
    y_ret = retention_bidirectional(q, k, v)
    mu = jnp.mean(y_ret, axis=-1, keepdims=True)
    var = jnp.mean(jnp.square(y_ret - mu), axis=-1, keepdims=True)
    y_ret = ((y_ret - mu) * lax.rsqrt(var + 1e-5)).reshape(B_, S_, RET_WIDTH)
    y_ret = jax.nn.silu(g.astype(f32)) * y_ret
    y_hy = hyena_bidirectional(hy_u, hy_conv_w, hy_conv_b, hy_filters, hy_bias)
    y_lru = rglru_bidirectional(lru_g, lru_x, lru_conv_w, lru_conv_b, lru_wa, lru_ba, lru_wx, lru_bx, lru_lambda)
    g_ret, g_hy, g_lru = jnp.split(mix_gain, [R, R + Hy])
    y = jnp.concatenate([rms_norm(y_ret, g_ret), rms_norm(y_hy, g_hy), rms_norm(y_lru, g_lru)], axis=-1)
    return y.astype(x.dtype) @ w_out


def setup_inputs(seed: int = 0) -> dict:
    key = jax.random.key(seed)
    ks = jax.random.split(key, 32)
    f32 = jnp.float32

    def nrm(k, shape, scale):
        return jax.random.normal(k, shape, f32) * scale

    L = DEPTH
    HID = HYENA_FILTER_HIDDEN
    x = nrm(ks[0], (BATCH, SEQ, D_MODEL), 1.0)
    ffa_w_gate = nrm(ks[1], (L, D_MODEL, D_FF), D_MODEL ** -0.5)
    ffa_w_up = nrm(ks[2], (L, D_MODEL, D_FF), D_MODEL ** -0.5)
    ffa_w_down = nrm(ks[3], (L, D_FF, D_MODEL), D_FF ** -0.5 * DEEPNORM_BETA)
    ffb_w_gate = nrm(ks[4], (L, D_MODEL, D_FF), D_MODEL ** -0.5)
    ffb_w_up = nrm(ks[5], (L, D_MODEL, D_FF), D_MODEL ** -0.5)
    ffb_w_down = nrm(ks[6], (L, D_FF, D_MODEL), D_FF ** -0.5 * DEEPNORM_BETA)
    ln_gain = 1.0 + nrm(ks[7], (L, 3, D_MODEL), 0.01)
    ln_bias = nrm(ks[8], (L, 3, D_MODEL), 0.01)
    w_in = nrm(ks[9], (L, D_MODEL, IN_WIDTH), D_MODEL ** -0.5)
    w_out = nrm(ks[10], (L, MIX_WIDTH, D_MODEL), MIX_WIDTH ** -0.5 * DEEPNORM_BETA)
    mix_norm_gain = 1.0 + nrm(ks[11], (L, MIX_WIDTH), 0.01)
    hy_conv_w = nrm(ks[12], (L, HYENA_SHORT_CONV, 3 * HYENA_WIDTH), HYENA_SHORT_CONV ** -0.5)
    hy_conv_b = nrm(ks[13], (L, 3 * HYENA_WIDTH), 0.01)
    hy_filt_w1 = nrm(ks[14], (L, HYENA_EMB_DIM, HID), HYENA_EMB_DIM ** -0.5)
    hy_filt_b1 = nrm(ks[15], (L, HID), 0.01)
    hy_filt_freq1 = 1.0 + nrm(ks[16], (L, HID), 0.01)
    hy_filt_w2 = nrm(ks[17], (L, HID, HID), HID ** -0.5)
    hy_filt_b2 = nrm(ks[18], (L, HID), 0.01)
    hy_filt_freq2 = 1.0 + nrm(ks[19], (L, HID), 0.01)
    hy_filt_w3 = nrm(ks[20], (L, HID, 2 * HYENA_ORDER * HYENA_WIDTH), HID ** -0.5)
    max_decay = -math.log(HYENA_DECAY_TARGET) / HYENA_FAST_DECAY_PCT
    min_decay = -math.log(HYENA_DECAY_TARGET) / HYENA_SLOW_DECAY_PCT
    base = jnp.linspace(min_decay, max_decay, HYENA_WIDTH, dtype=f32)
    hy_decay = jnp.tile(base, (L, 2 * HYENA_ORDER)) * (1.0 + nrm(ks[21], (L, 2 * HYENA_ORDER * HYENA_WIDTH), 0.01))
    hy_bias = nrm(ks[22], (L, HYENA_ORDER, HYENA_WIDTH), 1.0)
    lru_conv_w = nrm(ks[23], (L, LRU_CONV, LRU_WIDTH), LRU_CONV ** -0.5)
    lru_conv_b = nrm(ks[24], (L, LRU_WIDTH), 0.01)
    lru_wa = nrm(ks[25], (L, 2, LRU_BLOCKS, LRU_BLOCK_DIM, LRU_BLOCK_DIM), LRU_BLOCK_DIM ** -0.5)
    lru_ba = nrm(ks[26], (L, 2, LRU_WIDTH), 0.01)
    lru_wx = nrm(ks[27], (L, 2, LRU_BLOCKS, LRU_BLOCK_DIM, LRU_BLOCK_DIM), LRU_BLOCK_DIM ** -0.5)
    lru_bx = nrm(ks[28], (L, 2, LRU_WIDTH), 0.01)
    u = jax.random.uniform(ks[29], (L, 2, LRU_WIDTH), f32, 0.9, 0.999)
    s = u ** (1.0 / LRU_C)
    lru_lambda = jnp.log(s) - jnp.log1p(-s)
    return {"x": x, "ffa_w_gate": ffa_w_gate, "ffa_w_up": ffa_w_up, "ffa_w_down": ffa_w_down,
            "ffb_w_gate": ffb_w_gate, "ffb_w_up": ffb_w_up, "ffb_w_down": ffb_w_down,
            "ln_gain": ln_gain, "ln_bias": ln_bias, "w_in": w_in, "w_out": w_out,
            "mix_norm_gain": mix_norm_gain, "hy_conv_w": hy_conv_w, "hy_conv_b": hy_conv_b,
            "hy_filt_w1": hy_filt_w1, "hy_filt_b1": hy_filt_b1, "hy_filt_freq1": hy_filt_freq1,
            "hy_filt_w2": hy_filt_w2, "hy_filt_b2": hy_filt_b2, "hy_filt_freq2": hy_filt_freq2,
            "hy_filt_w3": hy_filt_w3, "hy_decay": hy_decay, "hy_bias": hy_bias,
            "lru_conv_w": lru_conv_w, "lru_conv_b": lru_conv_b, "lru_wa": lru_wa, "lru_ba": lru_ba,
            "lru_wx": lru_wx, "lru_bx": lru_bx, "lru_lambda": lru_lambda}


def reference(x, ffa_w_gate, ffa_w_up, ffa_w_down, ffb_w_gate, ffb_w_up, ffb_w_down,
              ln_gain, ln_bias, w_in, w_out, mix_norm_gain, hy_conv_w, hy_conv_b,
              hy_filt_w1, hy_filt_b1, hy_filt_freq1, hy_filt_w2, hy_filt_b2, hy_filt_freq2,
              hy_filt_w3, hy_decay, hy_bias, lru_conv_w, lru_conv_b, lru_wa, lru_ba,
              lru_wx, lru_bx, lru_lambda):
    S_ = x.shape[1]
    pos = jnp.arange(S_, dtype=jnp.int32)
    alpha = DEEPNORM_ALPHA
    for l in range(DEPTH):
        x = layer_norm(alpha * x + 0.5 * swiglu(x, ffa_w_gate[l], ffa_w_up[l], ffa_w_down[l]), ln_gain[l, 0], ln_bias[l, 0])
        filters = hyena_filters(S_, hy_filt_w1[l], hy_filt_b1[l], hy_filt_freq1[l], hy_filt_w2[l],
                                hy_filt_b2[l], hy_filt_freq2[l], hy_filt_w3[l], hy_decay[l])
        mix = hybrid_mixer(x, pos, w_in[l], w_out[l], mix_norm_gain[l], hy_conv_w[l], hy_conv_b[l],
                           filters, hy_bias[l], lru_conv_w[l], lru_conv_b[l], lru_wa[l], lru_ba[l],
                           lru_wx[l], lru_bx[l], lru_lambda[l])
        x = layer_norm(alpha * x + mix, ln_gain[l, 1], ln_bias[l, 1])
        x = layer_norm(alpha * x + 0.5 * swiglu(x, ffb_w_gate[l], ffb_w_up[l], ffb_w_down[l]), ln_gain[l, 2], ln_bias[l, 2])
    return x
```

```python
import functools
import math

import numpy as np
import jax
import jax.numpy as jnp
from jax import lax
from jax.experimental import pallas as pl
from jax.experimental.pallas import tpu as pltpu

F32 = jnp.float32
BF16 = jnp.bfloat16

LANES = 128
RET_HEAD_DIM = 256
LRU_BLOCK_DIM = 128
LRU_C = 8.0
FFT_N2 = 128
VMEM_CAP = 60 * 1024 * 1024


def _params(semantics, vmem_bytes):
    return pltpu.CompilerParams(
        dimension_semantics=semantics,
        vmem_limit_bytes=int(min(VMEM_CAP, max(vmem_bytes, 16 * 1024 * 1024))))


def _layer_norm_rows(y, g, b, eps=1e-5):
    mu = jnp.mean(y, axis=-1, keepdims=True)
    d = y - mu
    var = jnp.mean(d * d, axis=-1, keepdims=True)
    return d * lax.rsqrt(var + eps) * g + b


def _silu(x):
    return x * jax.nn.sigmoid(x)


def _ffn_body(x_ref, wg_ref, wu_ref, wd_ref, g_ref, b_ref, o_ref, xb_ref, *, alpha, rows, slab):
    j = pl.program_id(1)
    nj = pl.num_programs(1)
    tm = x_ref.shape[0]

    D = o_ref.shape[1]

    @pl.when(j == 0)
    def _():
        xb_ref[...] = x_ref[...].astype(BF16)
        o_ref[...] = jnp.zeros_like(o_ref)

    xb = xb_ref[...]
    gate = jnp.dot(xb, wg_ref[...], preferred_element_type=F32)
    up = jnp.dot(xb, wu_ref[...], preferred_element_type=F32)
    h = (_silu(gate) * up).astype(BF16)
    for c0 in range(0, D, slab):
        o_ref[:, c0:c0 + slab] += jnp.dot(h, wd_ref[:, c0:c0 + slab], preferred_element_type=F32)

    @pl.when(j == nj - 1)
    def _():
        g = g_ref[...]
        b = b_ref[...]

        def body(r, carry):
            sl = pl.ds(pl.multiple_of(r * rows, rows), rows)
            y = alpha * x_ref[sl, :] + 0.5 * o_ref[sl, :]
            o_ref[sl, :] = _layer_norm_rows(y, g, b)
            return carry

        lax.fori_loop(0, tm // rows, body, 0)


def _ffn(x, wg, wu, wd, g, b, alpha):
    M, D = x.shape
    F = wg.shape[1]
    tm = min(512, M)
    tf = 256 if F % 256 == 0 else F
    rows = min(16, tm)
    slab = 1024 if D % 1024 == 0 else D
    vmem = (2 * tm * D * 4 + tm * D * 2 + 2 * tm * D * 4 + 2 * 3 * D * tf * 2 + 6 * tm * tf * 4
            + 2 * tm * slab * 4)
    return pl.pallas_call(
        functools.partial(_ffn_body, alpha=alpha, rows=rows, slab=slab),
        grid=(M // tm, F // tf),
        in_specs=[
            pl.BlockSpec((tm, D), lambda i, j: (i, 0)),
            pl.BlockSpec((D, tf), lambda i, j: (0, j)),
            pl.BlockSpec((D, tf), lambda i, j: (0, j)),
            pl.BlockSpec((tf, D), lambda i, j: (j, 0)),
            pl.BlockSpec((1, D), lambda i, j: (0, 0)),
            pl.BlockSpec((1, D), lambda i, j: (0, 0)),
        ],
        out_specs=pl.BlockSpec((tm, D), lambda i, j: (i, 0)),
        out_shape=jax.ShapeDtypeStruct((M, D), F32),
        scratch_shapes=[pltpu.VMEM((tm, D), BF16)],
        compiler_params=_params(("parallel", "arbitrary"), vmem + (4 << 20)),
        name="ffn",
    )(x, wg, wu, wd, g, b)


def _inproj_body(x_ref, w_ref, o_ref, xb_ref):
    @pl.when(pl.program_id(1) == 0)
    def _():
        xb_ref[...] = x_ref[...].astype(BF16)

    o_ref[...] = jnp.dot(xb_ref[...], w_ref[...], preferred_element_type=F32)


def _inproj(x, w):
    M, K = x.shape
    N = w.shape[1]
    tm = min(512, M)
    tn = 1024 if N % 1024 == 0 else (512 if N % 512 == 0 else 128)
    vmem = 2 * tm * K * 4 + tm * K * 2 + 2 * K * tn * 2 + 2 * tm * tn * 4
    return pl.pallas_call(
        _inproj_body,
        grid=(M // tm, N // tn),
        in_specs=[
            pl.BlockSpec((tm, K), lambda i, j: (i, 0)),
            pl.BlockSpec((K, tn), lambda i, j: (0, j)),
        ],
        out_specs=pl.BlockSpec((tm, tn), lambda i, j: (i, j)),
        out_shape=jax.ShapeDtypeStruct((M, N), F32),
        scratch_shapes=[pltpu.VMEM((tm, K), BF16)],
        compiler_params=_params(("parallel", "arbitrary"), vmem + (4 << 20)),
        name="inproj",
    )(x, w)


def _outproj_body(y_ref, w_ref, x_ref, g_ref, b_ref, o_ref, *, alpha, rows, nj, tn):
    j = pl.program_id(1)
    tm = y_ref.shape[0]
    part = jnp.dot(y_ref[...], w_ref[...], preferred_element_type=F32)
    for jj in range(nj):
        @pl.when(j == jj)
        def _(jj=jj):
            o_ref[:, jj * tn:(jj + 1) * tn] = part

    @pl.when(j == nj - 1)
    def _():
        g = g_ref[...]
        b = b_ref[...]

        def body(r, carry):
            sl = pl.ds(pl.multiple_of(r * rows, rows), rows)
            y = alpha * x_ref[sl, :] + o_ref[sl, :]
            o_ref[sl, :] = _layer_norm_rows(y, g, b)
            return carry

        lax.fori_loop(0, tm // rows, body, 0)


def _outproj(y, w, x, g, b, alpha):
    M, K = y.shape
    D = w.shape[1]
    tm = min(256, M)
    tn = 1024 if D % 1024 == 0 else D
    nj = D // tn
    rows = min(16, tm)
    vmem = 2 * tm * K * 2 + 2 * K * tn * 2 + 4 * tm * D * 4 + 2 * tm * tn * 4
    return pl.pallas_call(
        functools.partial(_outproj_body, alpha=alpha, rows=rows, nj=nj, tn=tn),
        grid=(M // tm, nj),
        in_specs=[
            pl.BlockSpec((tm, K), lambda i, j: (i, 0)),
            pl.BlockSpec((K, tn), lambda i, j: (0, j)),
            pl.BlockSpec((tm, D), lambda i, j: (i, 0)),
            pl.BlockSpec((1, D), lambda i, j: (0, 0)),
            pl.BlockSpec((1, D), lambda i, j: (0, 0)),
        ],
        out_specs=pl.BlockSpec((tm, D), lambda i, j: (i, 0)),
        out_shape=jax.ShapeDtypeStruct((M, D), F32),
        compiler_params=_params(("parallel", "arbitrary"), vmem + (4 << 20)),
        name="outproj",
    )(y, w, x, g, b)


def _rotary(x, cos, sin):
    half = x.shape[-1] // 2
    x1 = x[:, :half]
    x2 = x[:, half:]
    return jnp.concatenate([x1 * cos - x2 * sin, x1 * sin + x2 * cos], axis=-1)


def _ret_qkv(q_ref, k_ref, cos_ref, sin_ref):
    cos = cos_ref[...]
    sin = sin_ref[...]
    q = _rotary(q_ref[...], cos, sin)
    k = _rotary(k_ref[...], cos, sin) * (RET_HEAD_DIM ** -0.5)
    return q, k


def _row_index(shape):
    return lax.broadcasted_iota(jnp.int32, shape, 0).astype(F32)


def _ret_fwd_body(q_ref, k_ref, v_ref, cos_ref, sin_ref, lg_ref, y_ref, state_ref):
    C, dk = q_ref.shape

    @pl.when(pl.program_id(2) == 0)
    def _():
        state_ref[...] = jnp.zeros_like(state_ref)

    lg = lg_ref[0:1, :dk]
    q, k = _ret_qkv(q_ref, k_ref, cos_ref, sin_ref)
    vb = v_ref[...].astype(BF16)
    row = _row_index((C, dk))
    q_in = (q * jnp.exp(lg * (row + 1.0))).astype(BF16)
    k_out = (k * jnp.exp(lg * (C - 1.0 - row))).astype(BF16)
    ii = lax.broadcasted_iota(jnp.int32, (C, C), 0)
    jj = lax.broadcasted_iota(jnp.int32, (C, C), 1)
    intra = jnp.exp(lg_ref[0:1, :C] * jnp.abs(ii - jj).astype(F32))
    s = lax.dot_general(q.astype(BF16), k.astype(BF16), (((1,), (1,)), ((), ())),
                        preferred_element_type=F32) * intra
    state = state_ref[...]
    y = jnp.dot(s.astype(BF16), vb, preferred_element_type=F32)
    y = y + jnp.dot(q_in, state.astype(BF16), preferred_element_type=F32)
    y_ref[...] = y
    kv = lax.dot_general(k_out, vb, (((0,), (0,)), ((), ())), preferred_element_type=F32)
    state_ref[...] = state * jnp.exp(lg * float(C)) + kv


def _ret_bwd_body(q_ref, k_ref, v_ref, g_ref, cos_ref, sin_ref, lg_ref, yf_ref, y_ref, state_ref):
    C, dk = q_ref.shape

    @pl.when(pl.program_id(2) == 0)
    def _():
        state_ref[...] = jnp.zeros_like(state_ref)

    lg = lg_ref[0:1, :dk]
    q, k = _ret_qkv(q_ref, k_ref, cos_ref, sin_ref)
    vb = v_ref[...].astype(BF16)
    row = _row_index((C, dk))
    q_in = (q * jnp.exp(lg * (C - row))).astype(BF16)
    k_out = (k * jnp.exp(lg * row)).astype(BF16)
    state = state_ref[...]
    y = yf_ref[...] + jnp.dot(q_in, state.astype(BF16), preferred_element_type=F32)
    kv = lax.dot_general(k_out, vb, (((0,), (0,)), ((), ())), preferred_element_type=F32)
    state_ref[...] = state * jnp.exp(lg * float(C)) + kv
    mu = jnp.mean(y, axis=-1, keepdims=True)
    d = y - mu
    var = jnp.mean(d * d, axis=-1, keepdims=True)
    y_ref[...] = _silu(g_ref[...]) * (d * lax.rsqrt(var + 1e-5))


def _retention(p3, cos, sin, lg_tab, H):
    B, S, _ = p3.shape
    dk = RET_HEAD_DIM
    C = min(512, S)
    nc = S // C
    W = lg_tab.shape[-1]
    blk = lambda off, rev: pl.BlockSpec(
        (None, C, dk),
        (lambda b, h, c: (b, nc - 1 - c, off + h)) if rev else (lambda b, h, c: (b, c, off + h)))
    tab = lambda rev: pl.BlockSpec(
        (C, dk // 2), (lambda b, h, c: (nc - 1 - c, 0)) if rev else (lambda b, h, c: (c, 0)))
    lg_spec = pl.BlockSpec((None, 8, W), lambda b, h, c: (h, 0, 0))
    vmem = 32 << 20
    y_f = pl.pallas_call(
        _ret_fwd_body,
        grid=(B, H, nc),
        in_specs=[blk(0, False), blk(H, False), blk(2 * H, False), tab(False), tab(False), lg_spec],
        out_specs=pl.BlockSpec((None, C, dk), lambda b, h, c: (b, c, h)),
        out_shape=jax.ShapeDtypeStruct((B, S, H * dk), F32),
        scratch_shapes=[pltpu.VMEM((dk, dk), F32)],
        compiler_params=_params(("parallel", "parallel", "arbitrary"), vmem),
        name="ret_fwd",
    )(p3, p3, p3, cos, sin, lg_tab)
    return pl.pallas_call(
        _ret_bwd_body,
        grid=(B, H, nc),
        in_specs=[blk(0, True), blk(H, True), blk(2 * H, True), blk(3 * H, True),
                  tab(True), tab(True), lg_spec,
                  pl.BlockSpec((None, C, dk), lambda b, h, c: (b, nc - 1 - c, h))],
        out_specs=pl.BlockSpec((None, C, dk), lambda b, h, c: (b, nc - 1 - c, h)),
        out_shape=jax.ShapeDtypeStruct((B, S, H * dk), F32),
        scratch_shapes=[pltpu.VMEM((dk, dk), F32)],
        compiler_params=_params(("parallel", "parallel", "arbitrary"), vmem),
        name="ret_bwd",
    )(p3, p3, p3, p3, cos, sin, lg_tab, y_f)


HALO = 8


def _fill_haloed(xs_ref, x_ref, S):
    zeros = jnp.zeros((HALO, xs_ref.shape[1]), F32)
    xs_ref[0:HALO, :] = zeros
    xs_ref[S + HALO:S + 2 * HALO, :] = zeros
    xs_ref[HALO:S + HALO, :] = x_ref[...]


def _conv_rows(xs_ref, r0, R, taps, left, bias):
    ext = xs_ref[pl.ds(r0, R + 2 * HALO), :]
    n = R + 2 * HALO
    acc = None
    for j, w in enumerate(taps):
        d = j - left
        sh = ext if d == 0 else pltpu.roll(ext, (-d) % n, axis=0)
        term = w * sh[HALO:HALO + R, :]
        acc = term if acc is None else acc + term
    return acc + bias


def _hy_conv_body(u_ref, w_ref, b_ref, o_ref, xs_ref, *, S, R, K):
    _fill_haloed(xs_ref, u_ref, S)
    taps = [w_ref[j:j + 1, :] for j in range(K)]
    bias = b_ref[...]

    def body(i, carry):
        r0 = pl.multiple_of(i * R, R)
        o_ref[pl.ds(r0, R), :] = _conv_rows(xs_ref, r0, R, taps, (K - 1) // 2, bias)
        return carry

    lax.fori_loop(0, S // R, body, 0)


def _hy_conv(p3, col0, width, w, b):
    B, S, _ = p3.shape
    K = w.shape[0]
    cb = LANES
    nb = width // cb
    C = width // 3
    R = min(256, S)
    wp = jnp.zeros((8, width), F32).at[:K].set(w)
    off = col0 // cb
    per = C // cb
    return pl.pallas_call(
        functools.partial(_hy_conv_body, S=S, R=R, K=K),
        grid=(B, nb),
        in_specs=[
            pl.BlockSpec((None, S, cb), lambda b_, j: (b_, 0, off + j)),
            pl.BlockSpec((8, cb), lambda b_, j: (0, j)),
            pl.BlockSpec((1, cb), lambda b_, j: (0, j)),
        ],
        out_specs=pl.BlockSpec((None, None, S, cb), lambda b_, j: (j // per, b_, 0, j % per)),
        out_shape=jax.ShapeDtypeStruct((3, B, S, C), F32),
        scratch_shapes=[pltpu.VMEM((S + 2 * HALO, cb), F32)],
        compiler_params=_params(("parallel", "parallel"), 6 * S * cb * 4 + (8 << 20)),
        name="hy_conv",
    )(p3, wp, b.reshape(1, width))


def _hy_filter_body(z_ref, w1_ref, b1_ref, f1_ref, w2_ref, b2_ref, f2_ref, w3_ref, dec_ref,
                    o_ref, sum_ref, *, L, tr):
    i = pl.program_id(0)
    hp = lax.Precision.HIGHEST
    z = z_ref[...]
    h = jnp.sin(f1_ref[...] * (jnp.dot(z, w1_ref[...], precision=hp, preferred_element_type=F32) + b1_ref[...]))
    h = jnp.sin(f2_ref[...] * (jnp.dot(h, w2_ref[...], precision=hp, preferred_element_type=F32) + b2_ref[...]))
    h = jnp.dot(h, w3_ref[...], precision=hp, preferred_element_type=F32)
    t = z[:, 0:1]
    h = h * jnp.exp(-t * jnp.abs(dec_ref[...]))
    half = h.shape[1] // 2
    m = i * tr + lax.broadcasted_iota(jnp.int32, (tr, half), 0)
    filt = jnp.where(m < L, h[:, :half], jnp.where(m > L, h[:, half:], 0.0))
    o_ref[...] = filt

    @pl.when(i == 0)
    def _():
        sum_ref[...] = jnp.zeros_like(sum_ref)

    sum_ref[...] += jnp.sum(jnp.abs(filt), axis=0, keepdims=True)


def _hy_filter(zfeat, w1, b1, f1, w2, b2, f2, w3, dec, L):
    N, E = zfeat.shape
    hid = w1.shape[1]
    C4 = w3.shape[1]
    tr = min(256, N)
    full = lambda a: pl.BlockSpec(a.shape, lambda i: (0,) * a.ndim)
    args = (w1, b1, f1, w2, b2, f2, w3, dec)
    return pl.pallas_call(
        functools.partial(_hy_filter_body, L=L, tr=tr),
        grid=(N // tr,),
        in_specs=[pl.BlockSpec((tr, E), lambda i: (i, 0))] + [full(a) for a in args],
        out_specs=[pl.BlockSpec((tr, C4 // 2), lambda i: (i, 0)),
                   pl.BlockSpec((1, C4 // 2), lambda i: (0, 0))],
        out_shape=[jax.ShapeDtypeStruct((N, C4 // 2), F32),
                   jax.ShapeDtypeStruct((1, C4 // 2), F32)],
        compiler_params=_params(("arbitrary",), 32 << 20),
        name="hy_filter",
    )(zfeat, *args)


def _stage_cols_body(w_ref, x_ref, o_ref, *, nparts):
    if nparts == 1:
        x = x_ref[...]
    else:
        x = jnp.concatenate([x_ref[b] for b in range(nparts)], axis=0)
    o_ref[...] = jnp.dot(w_ref[...], x.astype(BF16), preferred_element_type=F32)


def _stage_cols(wmat, x):
    Mo, K = wmat.shape
    cols = x.shape[-1]
    tc = min(8192, cols)
    if x.ndim == 3:
        nparts = x.shape[0]
        xspec = pl.BlockSpec((nparts, x.shape[1], tc), lambda j: (0, 0, j))
    else:
        nparts = 1
        xspec = pl.BlockSpec((K, tc), lambda j: (0, j))
    vmem = 2 * (K + Mo) * tc * 4 + K * tc * 6 + (8 << 20)
    return pl.pallas_call(
        functools.partial(_stage_cols_body, nparts=nparts),
        grid=(cols // tc,),
        in_specs=[pl.BlockSpec((Mo, K), lambda j: (0, 0)), xspec],
        out_specs=pl.BlockSpec((Mo, tc), lambda j: (0, j)),
        out_shape=jax.ShapeDtypeStruct((Mo, cols), F32),
        compiler_params=_params(("parallel",), vmem),
        name="hy_stage_cols",
    )(wmat, x)


def _hy_spectrum_body(g_ref, x_ref, s_ref, o_ref, *, kb):
    inv = 1.0 / (s_ref[...] + 1e-6)
    n2 = x_ref.shape[2]
    for t in range(kb):
        x = jnp.concatenate([x_ref[0, t], x_ref[1, t]], axis=0).astype(BF16)
        y = jnp.dot(g_ref[t], x, preferred_element_type=F32) * inv
        o_ref[0, t] = y[:n2]
        o_ref[1, t] = y[n2:]


def _hy_spectrum(gmat, xa, asum):
    _, N1, n2, W = xa.shape
    kb = min(2, N1)
    blk = pl.BlockSpec((2, kb, n2, W), lambda i: (0, i, 0, 0))
    return pl.pallas_call(
        functools.partial(_hy_spectrum_body, kb=kb),
        grid=(N1 // kb,),
        in_specs=[pl.BlockSpec((kb, 2 * n2, 2 * n2), lambda i: (i, 0, 0)), blk,
                  pl.BlockSpec((1, W), lambda i: (0, 0))],
        out_specs=blk,
        out_shape=jax.ShapeDtypeStruct(xa.shape, F32),
        compiler_params=_params(("parallel",), 4 * 2 * kb * n2 * W * 4 * 2 + (8 << 20)),
        name="hy_spectrum",
    )(gmat, xa, asum)


def _hy_freq_body(g_ref, gi_ref, x_ref, h_ref, o_ref, *, kb):
    n2 = x_ref.shape[2]
    for t in range(kb):
        x = jnp.concatenate([x_ref[0, t], x_ref[1, t]], axis=0).astype(BF16)
        y = jnp.dot(g_ref[t], x, preferred_element_type=F32)
        yr, yi = y[:n2], y[n2:]
        hr, hi = h_ref[0, t], h_ref[1, t]
        z = jnp.concatenate([yr * hr - yi * hi, yr * hi + yi * hr], axis=0).astype(BF16)
        e = jnp.dot(gi_ref[t], z, preferred_element_type=F32)
        o_ref[0, t] = e[:n2]
        o_ref[1, t] = e[n2:]


def _hy_freq(gmat, gimat, xa, hspec, order, C):
    _, N1, n2, _ = xa.shape
    kb = min(4, N1)
    blk = pl.BlockSpec((2, kb, n2, C), lambda i: (0, i, 0, 0))
    gspec = pl.BlockSpec((kb, 2 * n2, 2 * n2), lambda i: (i, 0, 0))
    return pl.pallas_call(
        functools.partial(_hy_freq_body, kb=kb),
        grid=(N1 // kb,),
        in_specs=[gspec, gspec, blk,
                  pl.BlockSpec((2, kb, n2, C), lambda i: (0, i, 0, order))],
        out_specs=blk,
        out_shape=jax.ShapeDtypeStruct(xa.shape, F32),
        compiler_params=_params(("parallel",), 6 * 2 * kb * n2 * C * 4 + 16 * n2 * C * 4 + (8 << 20)),
        name="hy_freq",
    )(gmat, gimat, xa, hspec)


def _hy_gate_body(w_ref, e_ref, z_ref, x_ref, b_ref, o_ref):
    nh = z_ref.shape[1]
    conv = jnp.dot(w_ref[...], e_ref[...].astype(BF16), preferred_element_type=F32)
    bias = b_ref[...]
    for b in range(2):
        o_ref[b] = x_ref[b] * (conv[b * nh:(b + 1) * nh] + bias * z_ref[b])


def _hy_gate(w4, e, z, gate, bias_cols):
    B, nh, cols = z.shape
    K = e.shape[0]
    tc = min(4096, cols)
    nat = pl.BlockSpec((B, nh, tc), lambda j: (0, 0, j))
    C = bias_cols.shape[1]
    if tc >= C:
        btile = jnp.tile(bias_cols, (1, tc // C))
        bspec = pl.BlockSpec((1, tc), lambda j: (0, 0))
    else:
        btile = bias_cols
        bspec = pl.BlockSpec((1, tc), lambda j: (0, j % (C // tc)))
    vmem = 2 * K * tc * 4 + K * tc * 2 + 6 * B * nh * tc * 4 + 2 * B * nh * tc * 4 + (8 << 20)
    return pl.pallas_call(
        _hy_gate_body,
        grid=(cols // tc,),
        in_specs=[pl.BlockSpec(w4.shape, lambda j: (0, 0)),
                  pl.BlockSpec((K, tc), lambda j: (0, j)), nat, nat, bspec],
        out_specs=nat,
        out_shape=jax.ShapeDtypeStruct(z.shape, F32),
        compiler_params=_params(("parallel",), vmem),
        name="hy_gate",
    )(w4, e, z, gate, btile)


def _dft_tables(S):
    N = 2 * S
    n2 = FFT_N2
    N1 = N // n2
    nh = S // n2
    k1 = np.arange(N1)
    f1 = np.exp(-2j * np.pi * np.outer(k1, np.arange(N1)) / N1)
    fa = f1[:, :nh]
    wa = np.block([[fa.real, -fa.imag], [fa.imag, fa.real]])
    wa_real = np.concatenate([f1.real, f1.imag], axis=0)
    a2 = np.arange(n2)
    base = np.exp(-2j * np.pi * np.outer(a2, a2) / n2)
    tw = np.exp(-2j * np.pi * np.outer(k1, a2) / N)
    g = base[None, :, :] * tw[:, None, :]
    gi = np.conj(np.transpose(g, (0, 2, 1)))
    blockify = lambda c: np.concatenate(
        [np.concatenate([c.real, -c.imag], axis=-1), np.concatenate([c.imag, c.real], axis=-1)], axis=-2)
    f4 = np.exp(2j * np.pi * np.outer(np.arange(nh), k1) / N1) / N
    wd = np.block([[f4.real, -f4.imag], [f4.imag, f4.real]])
    cast = lambda a: jnp.asarray(a, dtype=F32).astype(BF16)
    return dict(N1=N1, nh=nh, wa=cast(wa), wa_real=cast(wa_real), g=cast(blockify(g)),
                gi=cast(blockify(gi)), wd=cast(wd))


def _hyena(p3, col0, C, tabs, conv_w, conv_b, filt_params, hy_bias, zfeat):
    B, S, _ = p3.shape
    N1, nh, n2 = tabs["N1"], tabs["nh"], FFT_N2
    u = _hy_conv(p3, col0, 3 * C, conv_w, conv_b)
    filt, asum = _hy_filter(zfeat, *filt_params, S)
    fa = _stage_cols(tabs["wa_real"], filt.reshape(N1, n2 * 2 * C))
    hspec = _hy_spectrum(tabs["g"], fa.reshape(2, N1, n2, 2 * C), asum)
    z = u[0].reshape(B, nh, n2 * C)
    for order in range(2):
        xa = _stage_cols(tabs["wa"], z)
        e = _hy_freq(tabs["g"], tabs["gi"], xa.reshape(2, N1, n2, C), hspec, order, C)
        z = _hy_gate(tabs["wd"], e.reshape(2 * N1, n2 * C), z,
                     u[1 + order].reshape(B, nh, n2 * C), hy_bias[order].reshape(1, C))
    return z.reshape(B, S, C)


def _scan_rows(a, b, reverse):
    R = a.shape[0]
    row = lax.broadcasted_iota(jnp.int32, a.shape, 0)
    d = 1
    while d < R:
        if reverse:
            a_s = pltpu.roll(a, R - d, axis=0)
            b_s = pltpu.roll(b, R - d, axis=0)
            m = row < R - d
        else:
            a_s = pltpu.roll(a, d, axis=0)
            b_s = pltpu.roll(b, d, axis=0)
            m = row >= d
        b = jnp.where(m, a * b_s + b, b)
        a = jnp.where(m, a * a_s, a)
        d *= 2
    return a, b


def _gelu_tanh(x):
    return 0.5 * x * (1.0 + jnp.tanh(0.7978845608028654 * (x + 0.044715 * x * x * x)))


def _lru_body(g_ref, x_ref, cw_ref, cb_ref, w_ref, bias_ref, lam_ref, o_ref,
              xs_ref, hf_ref, ab_ref, bb_ref, *, S, R, K):
    bd = x_ref.shape[1]
    _fill_haloed(xs_ref, x_ref, S)
    taps = [cw_ref[j:j + 1, :] for j in range(K)]
    cbias = cb_ref[...]
    w = w_ref[...]
    bias = bias_ref[...]
    nl = -lam_ref[...]
    sp = jnp.maximum(nl, 0.0) + jnp.log1p(jnp.exp(-jnp.abs(nl)))
    sp0 = sp[0:1, :]
    sp1 = sp[1:2, :]
    nchunk = S // R

    def gates(r, i, spd, xr):
        log_a = -LRU_C * jax.nn.sigmoid(r) * spd
        a = jnp.exp(log_a)
        b = jnp.sqrt(1.0 - a * a) * (jax.nn.sigmoid(i) * xr)
        return a, b

    def fwd(i, hc):
        r0 = pl.multiple_of(i * R, R)
        xr = _conv_rows(xs_ref, r0, R, taps, (K - 1) // 2, cbias)
        proj = jnp.dot(xr.astype(BF16), w, preferred_element_type=F32) + bias
        a0, b0 = gates(proj[:, 0:bd], proj[:, bd:2 * bd], sp0, xr)
        a1, b1 = gates(proj[:, 2 * bd:3 * bd], proj[:, 3 * bd:4 * bd], sp1, xr)
        ab_ref[pl.ds(r0, R), :] = a1
        bb_ref[pl.ds(r0, R), :] = b1
        ac, hl = _scan_rows(a0, b0, reverse=False)
        h = ac * hc + hl
        hf_ref[pl.ds(r0, R), :] = h
        return h[R - 1:R, :]

    lax.fori_loop(0, nchunk, fwd, jnp.zeros((1, bd), F32))

    def bwd(t, hc):
        r0 = pl.multiple_of((nchunk - 1 - t) * R, R)
        sl = pl.ds(r0, R)
        ac, hl = _scan_rows(ab_ref[sl, :], bb_ref[sl, :], reverse=True)
        h = ac * hc + hl
        o_ref[sl, :] = _gelu_tanh(g_ref[sl, :]) * (hf_ref[sl, :] + h)
        return h[0:1, :]

    lax.fori_loop(0, nchunk, bwd, jnp.zeros((1, bd), F32))


def _lru(p3, col_g, col_x, width, conv_w, conv_b, wa, ba, wx, bx, lam):
    B, S, _ = p3.shape
    bd = LRU_BLOCK_DIM
    nb = width // bd
    K = conv_w.shape[0]
    R = min(256, S)
    cw = jnp.zeros((8, width), F32).at[:K].set(conv_w)
    wcat = jnp.concatenate([wa[0], wx[0], wa[1], wx[1]], axis=-1).astype(BF16)
    bsplit = lambda v: v.reshape(nb, 1, bd)
    bcat = jnp.concatenate([bsplit(ba[0]), bsplit(bx[0]), bsplit(ba[1]), bsplit(bx[1])], axis=-1)
    lam8 = jnp.zeros((8, width), F32).at[:2].set(lam)
    og, ox = col_g // bd, col_x // bd
    return pl.pallas_call(
        functools.partial(_lru_body, S=S, R=R, K=K),
        grid=(B, nb),
        in_specs=[
            pl.BlockSpec((None, S, bd), lambda b, j: (b, 0, og + j)),
            pl.BlockSpec((None, S, bd), lambda b, j: (b, 0, ox + j)),
            pl.BlockSpec((8, bd), lambda b, j: (0, j)),
            pl.BlockSpec((1, bd), lambda b, j: (0, j)),
            pl.BlockSpec((None, bd, 4 * bd), lambda b, j: (j, 0, 0)),
            pl.BlockSpec((None, 1, 4 * bd), lambda b, j: (j, 0, 0)),
            pl.BlockSpec((8, bd), lambda b, j: (0, j)),
        ],
        out_specs=pl.BlockSpec((None, S, bd), lambda b, j: (b, 0, j)),
        out_shape=jax.ShapeDtypeStruct((B, S, width), F32),
        scratch_shapes=[pltpu.VMEM((S + 2 * HALO, bd), F32), pltpu.VMEM((S, bd), F32),
                        pltpu.VMEM((S, bd), F32), pltpu.VMEM((S, bd), F32)],
        compiler_params=_params(("parallel", "parallel"), 10 * S * bd * 4 + (12 << 20)),
        name="lru",
    )(p3, p3, cw, conv_b.reshape(1, width), wcat, bcat, lam8)


def _mixnorm_body(a_ref, b_ref, c_ref, g_ref, o_ref):
    off = 0
    for ref in (a_ref, b_ref, c_ref):
        y = ref[...]
        w = y.shape[1]
        ms = jnp.mean(y * y, axis=-1, keepdims=True)
        o_ref[:, off:off + w] = (y * lax.rsqrt(ms + 1e-6) * g_ref[:, off:off + w]).astype(BF16)
        off += w


def _mixnorm(ya, yb, yc, gain):
    M = ya.shape[0]
    widths = (ya.shape[1], yb.shape[1], yc.shape[1])
    D = sum(widths)
    tm = min(128, M)
    return pl.pallas_call(
        _mixnorm_body,
        grid=(M // tm,),
        in_specs=[pl.BlockSpec((tm, w), lambda i: (i, 0)) for w in widths]
        + [pl.BlockSpec((1, D), lambda i: (0, 0))],
        out_specs=pl.BlockSpec((tm, D), lambda i: (i, 0)),
        out_shape=jax.ShapeDtypeStruct((M, D), BF16),
        compiler_params=_params(("parallel",), 32 << 20),
        name="mixnorm",
    )(ya, yb, yc, gain)


def kernel(x, ffa_w_gate, ffa_w_up, ffa_w_down, ffb_w_gate, ffb_w_up, ffb_w_down, ln_gain, ln_bias,
           w_in, w_out, mix_norm_gain, hy_conv_w, hy_conv_b, hy_filt_w1, hy_filt_b1, hy_filt_freq1,
           hy_filt_w2, hy_filt_b2, hy_filt_freq2, hy_filt_w3, hy_decay, hy_bias, lru_conv_w,
           lru_conv_b, lru_wa, lru_ba, lru_wx, lru_bx, lru_lambda):
    B, S, D = x.shape
    depth = w_in.shape[0]
    M = B * S
    alpha = (2 * depth) ** 0.25
    R = D // 2
    H = R // RET_HEAD_DIM
    Hy = D // 4
    Lw = D // 4
    col_hy = 4 * R
    col_lg = col_hy + 3 * Hy
    col_lx = col_lg + Lw

    pos = jnp.arange(S, dtype=F32)
    half = RET_HEAD_DIM // 2
    inv_freq = 1.0 / (10000.0 ** jnp.linspace(0.0, 1.0, half, dtype=F32))
    ang = pos[:, None] * inv_freq[None, :]
    cos, sin = jnp.cos(ang), jnp.sin(ang)
    log_g = jnp.log1p(-jnp.exp2(-5.0 - jnp.arange(H, dtype=F32)))
    lg_tab = jnp.broadcast_to(log_g[:, None, None], (H, 8, max(512, RET_HEAD_DIM)))
    tabs = _dft_tables(S)
    m = jnp.arange(2 * S)
    fpos = jnp.where(m < S, m, 2 * S - m).astype(F32)
    emb = hy_filt_w1.shape[1]
    bands = (emb - 1) // 2
    fr = jnp.linspace(1e-4, bands - 1, bands, dtype=F32)
    wang = 2.0 * math.pi * fpos / S
    zfeat = jnp.concatenate([(fpos / max(S - 1, 1))[:, None], jnp.cos(wang[:, None] * fr),
                             -jnp.sin(wang[:, None] * fr)], axis=-1)
    epad = LANES
    zfeat = jnp.pad(zfeat, ((0, 0), (0, epad - emb)))

    bf = lambda w: w.astype(BF16)
    ffa = (bf(ffa_w_gate), bf(ffa_w_up), bf(ffa_w_down))
    ffb = (bf(ffb_w_gate), bf(ffb_w_up), bf(ffb_w_down))
    w_in_b, w_out_b = bf(w_in), bf(w_out)
    row = lambda v: v.reshape(1, -1)

    h = x.reshape(M, D)
    for l in range(depth):
        h = _ffn(h, ffa[0][l], ffa[1][l], ffa[2][l], row(ln_gain[l, 0]), row(ln_bias[l, 0]), alpha)
        p3 = _inproj(h, w_in_b[l]).reshape(B, S, -1)
        y_ret = _retention(p3, cos, sin, lg_tab, H)
        w1p = jnp.pad(hy_filt_w1[l], ((0, epad - emb), (0, 0)))
        filt_params = (w1p, row(hy_filt_b1[l]), row(hy_filt_freq1[l]), hy_filt_w2[l],
                       row(hy_filt_b2[l]), row(hy_filt_freq2[l]), hy_filt_w3[l], row(hy_decay[l]))
        y_hy = _hyena(p3, col_hy, Hy, tabs, hy_conv_w[l], hy_conv_b[l], filt_params, hy_bias[l], zfeat)
        y_lru = _lru(p3, col_lg, col_lx, Lw, lru_conv_w[l], lru_conv_b[l], lru_wa[l], lru_ba[l],
                     lru_wx[l], lru_bx[l], lru_lambda[l])
        yb = _mixnorm(y_ret.reshape(M, R), y_hy.reshape(M, Hy), y_lru.reshape(M, Lw),
                      row(mix_norm_gain[l]))
        h = _outproj(yb, w_out_b[l], h, row(ln_gain[l, 1]), row(ln_bias[l, 1]), alpha)
        h = _ffn(h, ffb[0][l], ffb[1][l], ffb[2][l], row(ln_gain[l, 2]), row(ln_bias[l, 2]), alpha)
    return h.reshape(B, S, D)
```

```python
import functools
import math

import numpy as np
import jax
import jax.numpy as jnp
from jax import lax
from jax.experimental import pallas as pl
from jax.experimental.pallas import tpu as pltpu

F32 = jnp.float32
BF16 = jnp.bfloat16

LANES = 128
RET_HEAD_DIM = 256
LRU_BLOCK_DIM = 128
LRU_C = 8.0
FFT_N2 = 128
VMEM_CAP = 60 * 1024 * 1024


def _params(semantics, vmem_bytes):
    return pltpu.CompilerParams(
        dimension_semantics=semantics,
        vmem_limit_bytes=int(min(VMEM_CAP, max(vmem_bytes, 16 * 1024 * 1024))))


def _layer_norm_rows(y, g, b, eps=1e-5):
    mu = jnp.mean(y, axis=-1, keepdims=True)
    d = y - mu
    var = jnp.mean(d * d, axis=-1, keepdims=True)
    return d * lax.rsqrt(var + eps) * g + b


def _silu(x):
    return x * jax.nn.sigmoid(x)


def _ffn_body(x_ref, wg_ref, wu_ref, wd_ref, g_ref, b_ref, o_ref, xb_ref, *, alpha, rows, slab):
    j = pl.program_id(1)
    nj = pl.num_programs(1)
    tm = x_ref.shape[0]

    D = o_ref.shape[1]

    @pl.when(j == 0)
    def _():
        xb_ref[...] = x_ref[...].astype(BF16)
        o_ref[...] = jnp.zeros_like(o_ref)

    xb = xb_ref[...]
    gate = jnp.dot(xb, wg_ref[...], preferred_element_type=F32)
    up = jnp.dot(xb, wu_ref[...], preferred_element_type=F32)
    h = (_silu(gate) * up).astype(BF16)
    for c0 in range(0, D, slab):
        o_ref[:, c0:c0 + slab] += jnp.dot(h, wd_ref[:, c0:c0 + slab], preferred_element_type=F32)

    @pl.when(j == nj - 1)
    def _():
        g = g_ref[...]
        b = b_ref[...]

        def body(r, carry):
            sl = pl.ds(pl.multiple_of(r * rows, rows), rows)
            y = alpha * x_ref[sl, :] + 0.5 * o_ref[sl, :]
            o_ref[sl, :] = _layer_norm_rows(y, g, b)
            return carry

        lax.fori_loop(0, tm // rows, body, 0)


def _cast_body(x_ref, o_ref):
    o_ref[...] = x_ref[...].astype(BF16)


def _to_bf16(w):
    L, R, C = w.shape
    tr = 256
    while tr > 8 and (R % tr or tr * C * 4 > (12 << 20)):
        tr //= 2
    spec = pl.BlockSpec((None, tr, C), lambda l, i: (l, i, 0))
    return pl.pallas_call(
        _cast_body,
        grid=(L, R // tr),
        in_specs=[spec],
        out_specs=spec,
        out_shape=jax.ShapeDtypeStruct(w.shape, BF16),
        compiler_params=_params(("parallel", "parallel"), 3 * tr * C * 4 + (8 << 20)),
        name="to_bf16",
    )(w)


def _ffn(x, wg, wu, wd, layer, g, b, alpha):
    M, D = x.shape
    F = wg.shape[2]
    tm = min(512, M)
    tf = 256 if F % 256 == 0 else F
    rows = min(16, tm)
    slab = 1024 if D % 1024 == 0 else D
    vmem = (2 * tm * D * 4 + tm * D * 2 + 2 * tm * D * 4 + 2 * 3 * D * tf * 2 + 6 * tm * tf * 4
            + 2 * tm * slab * 4)
    return pl.pallas_call(
        functools.partial(_ffn_body, alpha=alpha, rows=rows, slab=slab),
        grid=(M // tm, F // tf),
        in_specs=[
            pl.BlockSpec((tm, D), lambda i, j: (i, 0)),
            pl.BlockSpec((None, D, tf), lambda i, j: (layer, 0, j)),
            pl.BlockSpec((None, D, tf), lambda i, j: (layer, 0, j)),
            pl.BlockSpec((None, tf, D), lambda i, j: (layer, j, 0)),
            pl.BlockSpec((1, D), lambda i, j: (0, 0)),
            pl.BlockSpec((1, D), lambda i, j: (0, 0)),
        ],
        out_specs=pl.BlockSpec((tm, D), lambda i, j: (i, 0)),
        out_shape=jax.ShapeDtypeStruct((M, D), F32),
        scratch_shapes=[pltpu.VMEM((tm, D), BF16)],
        compiler_params=_params(("parallel", "arbitrary"), vmem + (4 << 20)),
        name="ffn",
    )(x, wg, wu, wd, g, b)


def _inproj_body(x_ref, w_ref, o_ref, xb_ref):
    @pl.when(pl.program_id(1) == 0)
    def _():
        xb_ref[...] = x_ref[...].astype(BF16)

    o_ref[...] = jnp.dot(xb_ref[...], w_ref[...], preferred_element_type=F32)


def _inproj(x, w, layer):
    M, K = x.shape
    N = w.shape[2]
    tm = min(512, M)
    tn = 1024 if N % 1024 == 0 else (512 if N % 512 == 0 else 128)
    vmem = 2 * tm * K * 4 + tm * K * 2 + 2 * K * tn * 2 + 2 * tm * tn * 4
    return pl.pallas_call(
        _inproj_body,
        grid=(M // tm, N // tn),
        in_specs=[
            pl.BlockSpec((tm, K), lambda i, j: (i, 0)),
            pl.BlockSpec((None, K, tn), lambda i, j: (layer, 0, j)),
        ],
        out_specs=pl.BlockSpec((tm, tn), lambda i, j: (i, j)),
        out_shape=jax.ShapeDtypeStruct((M, N), F32),
        scratch_shapes=[pltpu.VMEM((tm, K), BF16)],
        compiler_params=_params(("parallel", "arbitrary"), vmem + (4 << 20)),
        name="inproj",
    )(x, w)


def _outproj_body(y_ref, w_ref, x_ref, g_ref, b_ref, o_ref, *, alpha, rows, nj, tn):
    j = pl.program_id(1)
    tm = y_ref.shape[0]
    part = jnp.dot(y_ref[...], w_ref[...], preferred_element_type=F32)
    for jj in range(nj):
        @pl.when(j == jj)
        def _(jj=jj):
            o_ref[:, jj * tn:(jj + 1) * tn] = part

    @pl.when(j == nj - 1)
    def _():
        g = g_ref[...]
        b = b_ref[...]

        def body(r, carry):
            sl = pl.ds(pl.multiple_of(r * rows, rows), rows)
            y = alpha * x_ref[sl, :] + o_ref[sl, :]
            o_ref[sl, :] = _layer_norm_rows(y, g, b)
            return carry

        lax.fori_loop(0, tm // rows, body, 0)


def _outproj(y, w, layer, x, g, b, alpha):
    M, K = y.shape
    D = w.shape[2]
    tm = min(512, M)
    tn = 1024 if D % 1024 == 0 else D
    nj = D // tn
    rows = min(16, tm)
    vmem = 2 * tm * K * 2 + 2 * K * tn * 2 + 2 * tm * D * 4 + 2 * tm * tn * 4
    once = pl.Buffered(1)
    return pl.pallas_call(
        functools.partial(_outproj_body, alpha=alpha, rows=rows, nj=nj, tn=tn),
        grid=(M // tm, nj),
        in_specs=[
            pl.BlockSpec((tm, K), lambda i, j: (i, 0)),
            pl.BlockSpec((None, K, tn), lambda i, j: (layer, 0, j)),
            pl.BlockSpec((tm, D), lambda i, j: (i, 0), pipeline_mode=once),
            pl.BlockSpec((1, D), lambda i, j: (0, 0)),
            pl.BlockSpec((1, D), lambda i, j: (0, 0)),
        ],
        out_specs=pl.BlockSpec((tm, D), lambda i, j: (i, 0), pipeline_mode=once),
        out_shape=jax.ShapeDtypeStruct((M, D), F32),
        compiler_params=_params(("parallel", "arbitrary"), vmem + (4 << 20)),
        name="outproj",
    )(y, w, x, g, b)


def _rotary(x, cos, sin):
    half = x.shape[-1] // 2
    x1 = x[:, :half]
    x2 = x[:, half:]
    return jnp.concatenate([x1 * cos - x2 * sin, x1 * sin + x2 * cos], axis=-1)


def _ret_qkv(q_ref, k_ref, cos_ref, sin_ref):
    cos = cos_ref[...]
    sin = sin_ref[...]
    q = _rotary(q_ref[...], cos, sin)
    k = _rotary(k_ref[...], cos, sin) * (RET_HEAD_DIM ** -0.5)
    return q, k


def _row_index(shape):
    return lax.broadcasted_iota(jnp.int32, shape, 0).astype(F32)


def _ret_fwd_body(q_ref, k_ref, v_ref, cos_ref, sin_ref, lg_ref, y_ref, state_ref):
    C, dk = q_ref.shape

    @pl.when(pl.program_id(2) == 0)
    def _():
        state_ref[...] = jnp.zeros_like(state_ref)

    lg = lg_ref[0:1, :dk]
    q, k = _ret_qkv(q_ref, k_ref, cos_ref, sin_ref)
    vb = v_ref[...].astype(BF16)
    row = _row_index((C, dk))
    q_in = (q * jnp.exp(lg * (row + 1.0))).astype(BF16)
    k_out = (k * jnp.exp(lg * (C - 1.0 - row))).astype(BF16)
    ii = lax.broadcasted_iota(jnp.int32, (C, C), 0)
    jj = lax.broadcasted_iota(jnp.int32, (C, C), 1)
    intra = jnp.exp(lg_ref[0:1, :C] * jnp.abs(ii - jj).astype(F32))
    s = lax.dot_general(q.astype(BF16), k.astype(BF16), (((1,), (1,)), ((), ())),
                        preferred_element_type=F32) * intra
    state = state_ref[...]
    y = jnp.dot(s.astype(BF16), vb, preferred_element_type=F32)
    y = y + jnp.dot(q_in, state.astype(BF16), preferred_element_type=F32)
    y_ref[...] = y
    kv = lax.dot_general(k_out, vb, (((0,), (0,)), ((), ())), preferred_element_type=F32)
    state_ref[...] = state * jnp.exp(lg * float(C)) + kv


def _ret_bwd_body(q_ref, k_ref, v_ref, g_ref, cos_ref, sin_ref, lg_ref, yf_ref, y_ref, state_ref):
    C, dk = q_ref.shape

    @pl.when(pl.program_id(2) == 0)
    def _():
        state_ref[...] = jnp.zeros_like(state_ref)

    lg = lg_ref[0:1, :dk]
    q, k = _ret_qkv(q_ref, k_ref, cos_ref, sin_ref)
    vb = v_ref[...].astype(BF16)
    row = _row_index((C, dk))
    q_in = (q * jnp.exp(lg * (C - row))).astype(BF16)
    k_out = (k * jnp.exp(lg * row)).astype(BF16)
    state = state_ref[...]
    y = yf_ref[...] + jnp.dot(q_in, state.astype(BF16), preferred_element_type=F32)
    kv = lax.dot_general(k_out, vb, (((0,), (0,)), ((), ())), preferred_element_type=F32)
    state_ref[...] = state * jnp.exp(lg * float(C)) + kv
    mu = jnp.mean(y, axis=-1, keepdims=True)
    d = y - mu
    var = jnp.mean(d * d, axis=-1, keepdims=True)
    y_ref[...] = _silu(g_ref[...]) * (d * lax.rsqrt(var + 1e-5))


def _retention(p3, cos, sin, lg_tab, H):
    B, S, _ = p3.shape
    dk = RET_HEAD_DIM
    C = min(512, S)
    nc = S // C
    W = lg_tab.shape[-1]
    blk = lambda off, rev: pl.BlockSpec(
        (None, C, dk),
        (lambda b, h, c: (b, nc - 1 - c, off + h)) if rev else (lambda b, h, c: (b, c, off + h)))
    tab = lambda rev: pl.BlockSpec(
        (C, dk // 2), (lambda b, h, c: (nc - 1 - c, 0)) if rev else (lambda b, h, c: (c, 0)))
    lg_spec = pl.BlockSpec((None, 8, W), lambda b, h, c: (h, 0, 0))
    vmem = 32 << 20
    y_f = pl.pallas_call(
        _ret_fwd_body,
        grid=(B, H, nc),
        in_specs=[blk(0, False), blk(H, False), blk(2 * H, False), tab(False), tab(False), lg_spec],
        out_specs=pl.BlockSpec((None, C, dk), lambda b, h, c: (b, c, h)),
        out_shape=jax.ShapeDtypeStruct((B, S, H * dk), F32),
        scratch_shapes=[pltpu.VMEM((dk, dk), F32)],
        compiler_params=_params(("parallel", "parallel", "arbitrary"), vmem),
        name="ret_fwd",
    )(p3, p3, p3, cos, sin, lg_tab)
    return pl.pallas_call(
        _ret_bwd_body,
        grid=(B, H, nc),
        in_specs=[blk(0, True), blk(H, True), blk(2 * H, True), blk(3 * H, True),
                  tab(True), tab(True), lg_spec,
                  pl.BlockSpec((None, C, dk), lambda b, h, c: (b, nc - 1 - c, h))],
        out_specs=pl.BlockSpec((None, C, dk), lambda b, h, c: (b, nc - 1 - c, h)),
        out_shape=jax.ShapeDtypeStruct((B, S, H * dk), F32),
        scratch_shapes=[pltpu.VMEM((dk, dk), F32)],
        compiler_params=_params(("parallel", "parallel", "arbitrary"), vmem),
        name="ret_bwd",
    )(p3, p3, p3, p3, cos, sin, lg_tab, y_f)


HALO = 8


def _fill_haloed(xs_ref, x_ref, S):
    zeros = jnp.zeros((HALO, xs_ref.shape[1]), F32)
    xs_ref[0:HALO, :] = zeros
    xs_ref[S + HALO:S + 2 * HALO, :] = zeros
    xs_ref[HALO:S + HALO, :] = x_ref[...]


def _conv_rows(xs_ref, r0, R, taps, left, bias):
    ext = xs_ref[pl.ds(r0, R + 2 * HALO), :]
    n = R + 2 * HALO
    acc = None
    for j, w in enumerate(taps):
        d = j - left
        sh = ext if d == 0 else pltpu.roll(ext, (-d) % n, axis=0)
        term = w * sh[HALO:HALO + R, :]
        acc = term if acc is None else acc + term
    return acc + bias


def _hy_conv_body(u_ref, w_ref, b_ref, o_ref, xs_ref, *, S, R, K):
    _fill_haloed(xs_ref, u_ref, S)
    taps = [w_ref[j:j + 1, :] for j in range(K)]
    bias = b_ref[...]

    def body(i, carry):
        r0 = pl.multiple_of(i * R, R)
        o_ref[pl.ds(r0, R), :] = _conv_rows(xs_ref, r0, R, taps, (K - 1) // 2, bias)
        return carry

    lax.fori_loop(0, S // R, body, 0)


def _hy_conv(p3, col0, width, w, b):
    B, S, _ = p3.shape
    K = w.shape[0]
    cb = LANES
    nb = width // cb
    C = width // 3
    R = min(256, S)
    wp = jnp.zeros((8, width), F32).at[:K].set(w)
    off = col0 // cb
    per = C // cb
    return pl.pallas_call(
        functools.partial(_hy_conv_body, S=S, R=R, K=K),
        grid=(B, nb),
        in_specs=[
            pl.BlockSpec((None, S, cb), lambda b_, j: (b_, 0, off + j)),
            pl.BlockSpec((8, cb), lambda b_, j: (0, j)),
            pl.BlockSpec((1, cb), lambda b_, j: (0, j)),
        ],
        out_specs=pl.BlockSpec((None, None, S, cb), lambda b_, j: (j // per, b_, 0, j % per)),
        out_shape=jax.ShapeDtypeStruct((3, B, S, C), F32),
        scratch_shapes=[pltpu.VMEM((S + 2 * HALO, cb), F32)],
        compiler_params=_params(("parallel", "parallel"), 6 * S * cb * 4 + (8 << 20)),
        name="hy_conv",
    )(p3, wp, b.reshape(1, width))


def _hy_filter_body(z_ref, w1_ref, b1_ref, f1_ref, w2_ref, b2_ref, f2_ref, w3_ref, dec_ref,
                    o_ref, sum_ref, *, L, tr):
    i = pl.program_id(0)
    hp = lax.Precision.HIGHEST
    z = z_ref[...]
    h = jnp.sin(f1_ref[...] * (jnp.dot(z, w1_ref[...], precision=hp, preferred_element_type=F32) + b1_ref[...]))
    h = jnp.sin(f2_ref[...] * (jnp.dot(h, w2_ref[...], precision=hp, preferred_element_type=F32) + b2_ref[...]))
    h = jnp.dot(h, w3_ref[...], precision=hp, preferred_element_type=F32)
    t = z[:, 0:1]
    h = h * jnp.exp(-t * jnp.abs(dec_ref[...]))
    m = i * tr + lax.broadcasted_iota(jnp.int32, h.shape, 0)
    filt = jnp.where(m == L, 0.0, h)
    o_ref[...] = filt

    @pl.when(i == 0)
    def _():
        sum_ref[...] = jnp.zeros_like(sum_ref)

    sum_ref[...] += jnp.sum(jnp.abs(filt), axis=0, keepdims=True)


def _hy_filter(zfeat, w1, b1, f1, w2, b2, f2, w3, dec, L):
    N, E = zfeat.shape
    hid = w1.shape[1]
    C4 = w3.shape[1]
    tr = min(256, L)
    nt_half = L // tr
    full = lambda a: pl.BlockSpec(a.shape, lambda i: (0,) * a.ndim)
    side = lambda a: pl.BlockSpec((a.shape[0], C4 // 2), lambda i: (0, i // nt_half))
    args = (w1, b1, f1, w2, b2, f2, w3, dec)
    return pl.pallas_call(
        functools.partial(_hy_filter_body, L=L, tr=tr),
        grid=(N // tr,),
        in_specs=[pl.BlockSpec((tr, E), lambda i: (i, 0))] + [full(a) for a in args[:6]]
        + [side(w3), side(dec)],
        out_specs=[pl.BlockSpec((tr, C4 // 2), lambda i: (i, 0)),
                   pl.BlockSpec((1, C4 // 2), lambda i: (0, 0))],
        out_shape=[jax.ShapeDtypeStruct((N, C4 // 2), F32),
                   jax.ShapeDtypeStruct((1, C4 // 2), F32)],
        compiler_params=_params(("arbitrary",), 32 << 20),
        name="hy_filter",
    )(zfeat, *args)


def _stage_a_body(w_ref, x_ref, o_ref, *, nparts, rows, n2):
    N1 = w_ref.shape[0]
    w = w_ref[...]

    def body(r, carry):
        parts = [x_ref[p, pl.ds(r, rows, stride=n2), :] for p in range(nparts)]
        x = parts[0] if nparts == 1 else jnp.concatenate(parts, axis=0)
        o_ref[pl.ds(r, N1, stride=n2), :] = jnp.dot(w, x.astype(BF16), preferred_element_type=F32)
        return carry

    lax.fori_loop(0, n2, body, 0, unroll=8)


def _stage_a(wmat, x, group, nparts):
    _, N1, K = wmat.shape
    _, T, C = x.shape
    n2 = FFT_N2
    rows = T // n2
    assert K == nparts * rows
    cb = LANES
    vmem = 2 * nparts * T * cb * 4 + 2 * N1 * n2 * cb * 4 + (8 << 20)
    return pl.pallas_call(
        functools.partial(_stage_a_body, nparts=nparts, rows=rows, n2=n2),
        grid=(C // cb, 2),
        in_specs=[pl.BlockSpec((None, N1, K), lambda j, t: (t, 0, 0)),
                  pl.BlockSpec((nparts, T, cb), lambda j, t: (group, 0, j))],
        out_specs=pl.BlockSpec((None, N1 * n2, cb), lambda j, t: (t, 0, j)),
        out_shape=jax.ShapeDtypeStruct((2, N1 * n2, C), F32),
        compiler_params=_params(("parallel", "arbitrary"), vmem),
        name="hy_stage_a",
    )(wmat, x)


def _hy_freq_body(g_ref, gi_ref, x_ref, f_ref, s_ref, o_ref, *, kb):
    n2 = x_ref.shape[2]
    inv = 1.0 / (s_ref[...] + 1e-6)
    for t in range(kb):
        g = g_ref[t]
        x = jnp.concatenate([x_ref[0, t], x_ref[1, t]], axis=0).astype(BF16)
        f = jnp.concatenate([f_ref[0, t], f_ref[1, t]], axis=0).astype(BF16)
        y = jnp.dot(g, x, preferred_element_type=F32)
        h = jnp.dot(g, f, preferred_element_type=F32) * inv
        yr, yi = y[:n2], y[n2:]
        hr, hi = h[:n2], h[n2:]
        z = jnp.concatenate([yr * hr - yi * hi, yr * hi + yi * hr], axis=0).astype(BF16)
        e = jnp.dot(gi_ref[t], z, preferred_element_type=F32)
        o_ref[0, t] = e[:n2]
        o_ref[1, t] = e[n2:]


def _hy_freq(gmat, gimat, xa, fa, asum, order, C):
    _, N1, n2, _ = xa.shape
    kb = min(4, N1)
    blk = pl.BlockSpec((2, kb, n2, C), lambda i: (0, i, 0, 0))
    gspec = pl.BlockSpec((kb, 2 * n2, 2 * n2), lambda i: (i, 0, 0))
    return pl.pallas_call(
        functools.partial(_hy_freq_body, kb=kb),
        grid=(N1 // kb,),
        in_specs=[gspec, gspec, blk,
                  pl.BlockSpec((2, kb, n2, C), lambda i: (0, i, 0, order)),
                  pl.BlockSpec((1, C), lambda i: (0, order))],
        out_specs=blk,
        out_shape=jax.ShapeDtypeStruct(xa.shape, F32),
        compiler_params=_params(("parallel",), 6 * 2 * kb * n2 * C * 4 + 24 * n2 * C * 4 + (8 << 20)),
        name="hy_freq",
    )(gmat, gimat, xa, fa, asum)


def _stage_d_body(w_ref, e_ref, z_ref, x_ref, b_ref, o_ref, *, n2):
    t = pl.program_id(2)
    nh, N1 = w_ref.shape
    w = w_ref[...]
    bias = b_ref[...]

    def part(r):
        e = e_ref[pl.ds(r, N1, stride=n2), :].astype(BF16)
        return jnp.dot(w, e, preferred_element_type=F32)

    @pl.when(t == 0)
    def _():
        def body(r, carry):
            o_ref[pl.ds(r, nh, stride=n2), :] = part(r)
            return carry
        lax.fori_loop(0, n2, body, 0, unroll=8)

    @pl.when(t == 1)
    def _():
        def body(r, carry):
            rows = pl.ds(r, nh, stride=n2)
            conv = o_ref[rows, :] + part(r)
            o_ref[rows, :] = x_ref[rows, :] * (conv + bias * z_ref[rows, :])
            return carry
        lax.fori_loop(0, n2, body, 0, unroll=8)


def _stage_d(w4, e, z, z_off, gate, gate_off, bias, B):
    _, _, nh, N1 = w4.shape
    n2 = FFT_N2
    S, C = z.shape[1], z.shape[2]
    cb = LANES
    vmem = 2 * N1 * n2 * cb * 4 + 6 * S * cb * 4 + (8 << 20)
    return pl.pallas_call(
        functools.partial(_stage_d_body, n2=n2),
        grid=(C // cb, B, 2),
        in_specs=[pl.BlockSpec((None, None, nh, N1), lambda j, b, t: (b, t, 0, 0)),
                  pl.BlockSpec((None, N1 * n2, cb), lambda j, b, t: (t, 0, j)),
                  pl.BlockSpec((None, S, cb), lambda j, b, t: (z_off + b, 0, j)),
                  pl.BlockSpec((None, S, cb), lambda j, b, t: (gate_off + b, 0, j)),
                  pl.BlockSpec((1, cb), lambda j, b, t: (0, j))],
        out_specs=pl.BlockSpec((None, S, cb), lambda j, b, t: (b, 0, j)),
        out_shape=jax.ShapeDtypeStruct((B, S, C), F32),
        compiler_params=_params(("parallel", "parallel", "arbitrary"), vmem),
        name="hy_stage_d",
    )(w4, e, z, gate, bias)


def _dft_tables(S):
    N = 2 * S
    n2 = FFT_N2
    N1 = N // n2
    nh = S // n2
    k1 = np.arange(N1)
    f1 = np.exp(-2j * np.pi * np.outer(k1, np.arange(N1)) / N1)
    fa = f1[:, :nh]
    wa = np.stack([np.concatenate([fa.real, -fa.imag], axis=1),
                   np.concatenate([fa.imag, fa.real], axis=1)])
    wa_real = np.stack([f1.real, f1.imag])
    a2 = np.arange(n2)
    base = np.exp(-2j * np.pi * np.outer(a2, a2) / n2)
    tw = np.exp(-2j * np.pi * np.outer(k1, a2) / N)
    g = base[None, :, :] * tw[:, None, :]
    gi = np.conj(np.transpose(g, (0, 2, 1)))
    blockify = lambda c: np.concatenate(
        [np.concatenate([c.real, -c.imag], axis=-1), np.concatenate([c.imag, c.real], axis=-1)], axis=-2)
    f4 = np.exp(2j * np.pi * np.outer(np.arange(nh), k1) / N1) / N
    wd = np.stack([np.stack([f4.real, -f4.imag]), np.stack([f4.imag, f4.real])])
    cast = lambda a: jnp.asarray(a, dtype=F32).astype(BF16)
    return dict(N1=N1, nh=nh, wa=cast(wa), wa_real=cast(wa_real), g=cast(blockify(g)),
                gi=cast(blockify(gi)), wd=cast(wd))


def _hyena(p3, col0, C, tabs, conv_w, conv_b, filt_params, hy_bias, zfeat):
    B, S, _ = p3.shape
    N1, nh, n2 = tabs["N1"], tabs["nh"], FFT_N2
    del nh
    u = _hy_conv(p3, col0, 3 * C, conv_w, conv_b).reshape(3 * B, S, C)
    filt, asum = _hy_filter(zfeat, *filt_params, S)
    fa = _stage_a(tabs["wa_real"], filt.reshape(1, 2 * S, 2 * C), 0, 1).reshape(2, N1, n2, 2 * C)
    z = u
    for order in range(2):
        xa = _stage_a(tabs["wa"], z, 0, B)
        e = _hy_freq(tabs["g"], tabs["gi"], xa.reshape(2, N1, n2, C), fa, asum, order, C)
        z = _stage_d(tabs["wd"], e.reshape(2, N1 * n2, C), z, 0, u, B * (1 + order),
                     hy_bias[order].reshape(1, C), B)
    return z


def _scan_rows(a, b, reverse):
    R = a.shape[0]
    row = lax.broadcasted_iota(jnp.int32, a.shape, 0)
    d = 1
    while d < R:
        if reverse:
            a_s = pltpu.roll(a, R - d, axis=0)
            b_s = pltpu.roll(b, R - d, axis=0)
            m = row < R - d
        else:
            a_s = pltpu.roll(a, d, axis=0)
            b_s = pltpu.roll(b, d, axis=0)
            m = row >= d
        b = jnp.where(m, a * b_s + b, b)
        a = jnp.where(m, a * a_s, a)
        d *= 2
    return a, b


def _gelu_tanh(x):
    return 0.5 * x * (1.0 + jnp.tanh(0.7978845608028654 * (x + 0.044715 * x * x * x)))


def _lru_body(g_ref, x_ref, cw_ref, cb_ref, w_ref, bias_ref, lam_ref, o_ref,
              xs_ref, hf_ref, ab_ref, bb_ref, *, S, R, K):
    bd = x_ref.shape[1]
    _fill_haloed(xs_ref, x_ref, S)
    taps = [cw_ref[j:j + 1, :] for j in range(K)]
    cbias = cb_ref[...]
    w = w_ref[...]
    bias = bias_ref[...]
    nl = -lam_ref[...]
    sp = jnp.maximum(nl, 0.0) + jnp.log1p(jnp.exp(-jnp.abs(nl)))
    sp0 = sp[0:1, :]
    sp1 = sp[1:2, :]
    nchunk = S // R

    def gates(r, i, spd, xr):
        log_a = -LRU_C * jax.nn.sigmoid(r) * spd
        a = jnp.exp(log_a)
        b = jnp.sqrt(1.0 - a * a) * (jax.nn.sigmoid(i) * xr)
        return a, b

    def fwd(i, hc):
        r0 = pl.multiple_of(i * R, R)
        xr = _conv_rows(xs_ref, r0, R, taps, (K - 1) // 2, cbias)
        proj = jnp.dot(xr.astype(BF16), w, preferred_element_type=F32) + bias
        a0, b0 = gates(proj[:, 0:bd], proj[:, bd:2 * bd], sp0, xr)
        a1, b1 = gates(proj[:, 2 * bd:3 * bd], proj[:, 3 * bd:4 * bd], sp1, xr)
        ab_ref[pl.ds(r0, R), :] = a1
        bb_ref[pl.ds(r0, R), :] = b1
        ac, hl = _scan_rows(a0, b0, reverse=False)
        h = ac * hc + hl
        hf_ref[pl.ds(r0, R), :] = h
        return h[R - 1:R, :]

    lax.fori_loop(0, nchunk, fwd, jnp.zeros((1, bd), F32))

    def bwd(t, hc):
        r0 = pl.multiple_of((nchunk - 1 - t) * R, R)
        sl = pl.ds(r0, R)
        ac, hl = _scan_rows(ab_ref[sl, :], bb_ref[sl, :], reverse=True)
        h = ac * hc + hl
        o_ref[sl, :] = _gelu_tanh(g_ref[sl, :]) * (hf_ref[sl, :] + h)
        return h[0:1, :]

    lax.fori_loop(0, nchunk, bwd, jnp.zeros((1, bd), F32))


def _lru(p3, col_g, col_x, width, conv_w, conv_b, wa, ba, wx, bx, lam):
    B, S, _ = p3.shape
    bd = LRU_BLOCK_DIM
    nb = width // bd
    K = conv_w.shape[0]
    R = min(256, S)
    cw = jnp.zeros((8, width), F32).at[:K].set(conv_w)
    wcat = jnp.concatenate([wa[0], wx[0], wa[1], wx[1]], axis=-1).astype(BF16)
    bsplit = lambda v: v.reshape(nb, 1, bd)
    bcat = jnp.concatenate([bsplit(ba[0]), bsplit(bx[0]), bsplit(ba[1]), bsplit(bx[1])], axis=-1)
    lam8 = jnp.zeros((8, width), F32).at[:2].set(lam)
    og, ox = col_g // bd, col_x // bd
    return pl.pallas_call(
        functools.partial(_lru_body, S=S, R=R, K=K),
        grid=(B, nb),
        in_specs=[
            pl.BlockSpec((None, S, bd), lambda b, j: (b, 0, og + j)),
            pl.BlockSpec((None, S, bd), lambda b, j: (b, 0, ox + j)),
            pl.BlockSpec((8, bd), lambda b, j: (0, j)),
            pl.BlockSpec((1, bd), lambda b, j: (0, j)),
            pl.BlockSpec((None, bd, 4 * bd), lambda b, j: (j, 0, 0)),
            pl.BlockSpec((None, 1, 4 * bd), lambda b, j: (j, 0, 0)),
            pl.BlockSpec((8, bd), lambda b, j: (0, j)),
        ],
        out_specs=pl.BlockSpec((None, S, bd), lambda b, j: (b, 0, j)),
        out_shape=jax.ShapeDtypeStruct((B, S, width), F32),
        scratch_shapes=[pltpu.VMEM((S + 2 * HALO, bd), F32), pltpu.VMEM((S, bd), F32),
                        pltpu.VMEM((S, bd), F32), pltpu.VMEM((S, bd), F32)],
        compiler_params=_params(("parallel", "parallel"), 10 * S * bd * 4 + (12 << 20)),
        name="lru",
    )(p3, p3, cw, conv_b.reshape(1, width), wcat, bcat, lam8)


def _mixnorm_body(a_ref, b_ref, c_ref, g_ref, o_ref):
    off = 0
    for ref in (a_ref, b_ref, c_ref):
        y = ref[...]
        w = y.shape[1]
        ms = jnp.mean(y * y, axis=-1, keepdims=True)
        o_ref[:, off:off + w] = (y * lax.rsqrt(ms + 1e-6) * g_ref[:, off:off + w]).astype(BF16)
        off += w


def _mixnorm(ya, yb, yc, gain):
    M = ya.shape[0]
    widths = (ya.shape[1], yb.shape[1], yc.shape[1])
    D = sum(widths)
    tm = min(128, M)
    return pl.pallas_call(
        _mixnorm_body,
        grid=(M // tm,),
        in_specs=[pl.BlockSpec((tm, w), lambda i: (i, 0)) for w in widths]
        + [pl.BlockSpec((1, D), lambda i: (0, 0))],
        out_specs=pl.BlockSpec((tm, D), lambda i: (i, 0)),
        out_shape=jax.ShapeDtypeStruct((M, D), BF16),
        compiler_params=_params(("parallel",), 32 << 20),
        name="mixnorm",
    )(ya, yb, yc, gain)


def kernel(x, ffa_w_gate, ffa_w_up, ffa_w_down, ffb_w_gate, ffb_w_up, ffb_w_down, ln_gain, ln_bias,
           w_in, w_out, mix_norm_gain, hy_conv_w, hy_conv_b, hy_filt_w1, hy_filt_b1, hy_filt_freq1,
           hy_filt_w2, hy_filt_b2, hy_filt_freq2, hy_filt_w3, hy_decay, hy_bias, lru_conv_w,
           lru_conv_b, lru_wa, lru_ba, lru_wx, lru_bx, lru_lambda):
    B, S, D = x.shape
    depth = w_in.shape[0]
    M = B * S
    alpha = (2 * depth) ** 0.25
    R = D // 2
    H = R // RET_HEAD_DIM
    Hy = D // 4
    Lw = D // 4
    col_hy = 4 * R
    col_lg = col_hy + 3 * Hy
    col_lx = col_lg + Lw

    pos = jnp.arange(S, dtype=F32)
    half = RET_HEAD_DIM // 2
    inv_freq = 1.0 / (10000.0 ** jnp.linspace(0.0, 1.0, half, dtype=F32))
    ang = pos[:, None] * inv_freq[None, :]
    cos, sin = jnp.cos(ang), jnp.sin(ang)
    log_g = jnp.log1p(-jnp.exp2(-5.0 - jnp.arange(H, dtype=F32)))
    lg_tab = jnp.broadcast_to(log_g[:, None, None], (H, 8, max(512, RET_HEAD_DIM)))
    tabs = _dft_tables(S)
    m = jnp.arange(2 * S)
    fpos = jnp.where(m < S, m, 2 * S - m).astype(F32)
    emb = hy_filt_w1.shape[1]
    bands = (emb - 1) // 2
    fr = jnp.linspace(1e-4, bands - 1, bands, dtype=F32)
    wang = 2.0 * math.pi * fpos / S
    zfeat = jnp.concatenate([(fpos / max(S - 1, 1))[:, None], jnp.cos(wang[:, None] * fr),
                             -jnp.sin(wang[:, None] * fr)], axis=-1)
    epad = LANES
    zfeat = jnp.pad(zfeat, ((0, 0), (0, epad - emb)))

    assert B == 2, "the Hyena DFT carries the two batch rows as one complex signal"
    ffa = (_to_bf16(ffa_w_gate), _to_bf16(ffa_w_up), _to_bf16(ffa_w_down))
    ffb = (_to_bf16(ffb_w_gate), _to_bf16(ffb_w_up), _to_bf16(ffb_w_down))
    w_in_b, w_out_b = _to_bf16(w_in), _to_bf16(w_out)
    row = lambda v: v.reshape(1, -1)

    h = x.reshape(M, D)
    for l in range(depth):
        h = _ffn(h, *ffa, l, row(ln_gain[l, 0]), row(ln_bias[l, 0]), alpha)
        p3 = _inproj(h, w_in_b, l).reshape(B, S, -1)
        y_ret = _retention(p3, cos, sin, lg_tab, H)
        w1p = jnp.pad(hy_filt_w1[l], ((0, epad - emb), (0, 0)))
        filt_params = (w1p, row(hy_filt_b1[l]), row(hy_filt_freq1[l]), hy_filt_w2[l],
                       row(hy_filt_b2[l]), row(hy_filt_freq2[l]), hy_filt_w3[l], row(hy_decay[l]))
        y_hy = _hyena(p3, col_hy, Hy, tabs, hy_conv_w[l], hy_conv_b[l], filt_params, hy_bias[l], zfeat)
        y_lru = _lru(p3, col_lg, col_lx, Lw, lru_conv_w[l], lru_conv_b[l], lru_wa[l], lru_ba[l],
                     lru_wx[l], lru_bx[l], lru_lambda[l])
        yb = _mixnorm(y_ret.reshape(M, R), y_hy.reshape(M, Hy), y_lru.reshape(M, Lw),
                      row(mix_norm_gain[l]))
        h = _outproj(yb, w_out_b, l, h, row(ln_gain[l, 1]), row(ln_bias[l, 1]), alpha)
        h = _ffn(h, *ffb, l, row(ln_gain[l, 2]), row(ln_bias[l, 2]), alpha)
    return h.reshape(B, S, D)
```

```python
import functools
import math

import numpy as np
import jax
import jax.numpy as jnp
from jax import lax
from jax.experimental import pallas as pl
from jax.experimental.pallas import tpu as pltpu

F32 = jnp.float32
BF16 = jnp.bfloat16

LANES = 128
RET_HEAD_DIM = 256
LRU_BLOCK_DIM = 128
LRU_C = 8.0
VMEM_CAP = 60 * 1024 * 1024


def _params(semantics, vmem_bytes):
    return pltpu.CompilerParams(
        dimension_semantics=semantics,
        vmem_limit_bytes=int(min(VMEM_CAP, max(vmem_bytes, 16 * 1024 * 1024))))


def _layer_norm_rows(y, g, b, eps=1e-5):
    mu = jnp.mean(y, axis=-1, keepdims=True)
    d = y - mu
    var = jnp.mean(d * d, axis=-1, keepdims=True)
    return d * lax.rsqrt(var + eps) * g + b


def _silu(x):
    return x * jax.nn.sigmoid(x)


def _ffn_body(x_ref, wg_ref, wu_ref, wd_ref, g_ref, b_ref, o_ref, xb_ref, *, alpha, rows, slab):
    j = pl.program_id(1)
    nj = pl.num_programs(1)
    tm = x_ref.shape[0]

    D = o_ref.shape[1]

    @pl.when(j == 0)
    def _():
        xb_ref[...] = x_ref[...].astype(BF16)
        o_ref[...] = jnp.zeros_like(o_ref)

    xb = xb_ref[...]
    gate = jnp.dot(xb, wg_ref[...], preferred_element_type=F32)
    up = jnp.dot(xb, wu_ref[...], preferred_element_type=F32)
    h = (_silu(gate) * up).astype(BF16)
    for c0 in range(0, D, slab):
        o_ref[:, c0:c0 + slab] += jnp.dot(h, wd_ref[:, c0:c0 + slab], preferred_element_type=F32)

    @pl.when(j == nj - 1)
    def _():
        g = g_ref[...]
        b = b_ref[...]

        def body(r, carry):
            sl = pl.ds(pl.multiple_of(r * rows, rows), rows)
            y = alpha * x_ref[sl, :] + 0.5 * o_ref[sl, :]
            o_ref[sl, :] = _layer_norm_rows(y, g, b)
            return carry

        lax.fori_loop(0, tm // rows, body, 0)


def _cast_body(x_ref, o_ref):
    o_ref[...] = x_ref[...].astype(BF16)


def _to_bf16(w):
    L, R, C = w.shape
    tr = 256
    while tr > 8 and (R % tr or tr * C * 4 > (12 << 20)):
        tr //= 2
    spec = pl.BlockSpec((None, tr, C), lambda l, i: (l, i, 0))
    return pl.pallas_call(
        _cast_body,
        grid=(L, R // tr),
        in_specs=[spec],
        out_specs=spec,
        out_shape=jax.ShapeDtypeStruct(w.shape, BF16),
        compiler_params=_params(("parallel", "parallel"), 3 * tr * C * 4 + (8 << 20)),
        name="to_bf16",
    )(w)


def _ffn(x, wg, wu, wd, layer, g, b, alpha):
    M, D = x.shape
    F = wg.shape[2]
    tm = min(512, M)
    tf = 256 if F % 256 == 0 else F
    rows = min(16, tm)
    slab = 1024 if D % 1024 == 0 else D
    vmem = (2 * tm * D * 4 + tm * D * 2 + 2 * tm * D * 4 + 2 * 3 * D * tf * 2 + 6 * tm * tf * 4
            + 2 * tm * slab * 4)
    return pl.pallas_call(
        functools.partial(_ffn_body, alpha=alpha, rows=rows, slab=slab),
        grid=(M // tm, F // tf),
        in_specs=[
            pl.BlockSpec((tm, D), lambda i, j: (i, 0)),
            pl.BlockSpec((None, D, tf), lambda i, j: (layer, 0, j)),
            pl.BlockSpec((None, D, tf), lambda i, j: (layer, 0, j)),
            pl.BlockSpec((None, tf, D), lambda i, j: (layer, j, 0)),
            pl.BlockSpec((1, D), lambda i, j: (0, 0)),
            pl.BlockSpec((1, D), lambda i, j: (0, 0)),
        ],
        out_specs=pl.BlockSpec((tm, D), lambda i, j: (i, 0)),
        out_shape=jax.ShapeDtypeStruct((M, D), F32),
        scratch_shapes=[pltpu.VMEM((tm, D), BF16)],
        compiler_params=_params(("parallel", "arbitrary"), vmem + (4 << 20)),
        name="ffn",
    )(x, wg, wu, wd, g, b)


def _inproj_body(x_ref, w_ref, o_ref, xb_ref):
    @pl.when(pl.program_id(1) == 0)
    def _():
        xb_ref[...] = x_ref[...].astype(BF16)

    o_ref[...] = jnp.dot(xb_ref[...], w_ref[...], preferred_element_type=F32)


def _inproj(x, w, layer):
    M, K = x.shape
    N = w.shape[2]
    tm = min(512, M)
    tn = 1024 if N % 1024 == 0 else (512 if N % 512 == 0 else 128)
    vmem = 2 * tm * K * 4 + tm * K * 2 + 2 * K * tn * 2 + 2 * tm * tn * 4
    return pl.pallas_call(
        _inproj_body,
        grid=(M // tm, N // tn),
        in_specs=[
            pl.BlockSpec((tm, K), lambda i, j: (i, 0)),
            pl.BlockSpec((None, K, tn), lambda i, j: (layer, 0, j)),
        ],
        out_specs=pl.BlockSpec((tm, tn), lambda i, j: (i, j)),
        out_shape=jax.ShapeDtypeStruct((M, N), F32),
        scratch_shapes=[pltpu.VMEM((tm, K), BF16)],
        compiler_params=_params(("parallel", "arbitrary"), vmem + (4 << 20)),
        name="inproj",
    )(x, w)


def _outproj_body(y_ref, w_ref, x_ref, g_ref, b_ref, o_ref, *, alpha, rows, nj, slab):
    j = pl.program_id(1)
    tm, D = o_ref.shape
    rc = x_ref.shape[0]

    @pl.when(j == 0)
    def _():
        o_ref[...] = jnp.zeros_like(o_ref)

    yk = y_ref[...]
    for c0 in range(0, D, slab):
        o_ref[:, c0:c0 + slab] += jnp.dot(yk, w_ref[:, c0:c0 + slab], preferred_element_type=F32)
    sl = pl.ds(pl.multiple_of(j * rc, rc), rc)
    o_ref[sl, :] += alpha * x_ref[...]

    @pl.when(j == nj - 1)
    def _():
        g = g_ref[...]
        b = b_ref[...]

        def body(r, carry):
            rs = pl.ds(pl.multiple_of(r * rows, rows), rows)
            o_ref[rs, :] = _layer_norm_rows(o_ref[rs, :], g, b)
            return carry

        lax.fori_loop(0, tm // rows, body, 0)


def _outproj(y, w, layer, x, g, b, alpha):
    M, K = y.shape
    D = w.shape[2]
    tm = min(512, M)
    tk = 1024 if K % 1024 == 0 else K
    nj = K // tk
    rc = tm // nj
    rows = min(16, tm)
    slab = 1024 if D % 1024 == 0 else D
    vmem = 2 * tm * tk * 2 + 2 * tk * D * 2 + 2 * rc * D * 4 + 2 * tm * D * 4 + 2 * tm * slab * 4
    return pl.pallas_call(
        functools.partial(_outproj_body, alpha=alpha, rows=rows, nj=nj, slab=slab),
        grid=(M // tm, nj),
        in_specs=[
            pl.BlockSpec((tm, tk), lambda i, j: (i, j)),
            pl.BlockSpec((None, tk, D), lambda i, j: (layer, j, 0)),
            pl.BlockSpec((rc, D), lambda i, j: (i * nj + j, 0)),
            pl.BlockSpec((1, D), lambda i, j: (0, 0)),
            pl.BlockSpec((1, D), lambda i, j: (0, 0)),
        ],
        out_specs=pl.BlockSpec((tm, D), lambda i, j: (i, 0)),
        out_shape=jax.ShapeDtypeStruct((M, D), F32),
        compiler_params=_params(("parallel", "arbitrary"), vmem + (6 << 20)),
        name="outproj",
    )(y, w, x, g, b)


def _rotary(x, cos, sin):
    half = x.shape[-1] // 2
    x1 = x[:, :half]
    x2 = x[:, half:]
    return jnp.concatenate([x1 * cos - x2 * sin, x1 * sin + x2 * cos], axis=-1)


def _ret_qkv(q_ref, k_ref, cos_ref, sin_ref):
    cos = cos_ref[...]
    sin = sin_ref[...]
    q = _rotary(q_ref[...], cos, sin)
    k = _rotary(k_ref[...], cos, sin) * (RET_HEAD_DIM ** -0.5)
    return q, k


def _row_index(shape):
    return lax.broadcasted_iota(jnp.int32, shape, 0).astype(F32)


def _ret_fwd_body(q_ref, k_ref, v_ref, cos_ref, sin_ref, lg_ref, y_ref,
                  state_ref, qd_ref, kd_ref, intra_ref):
    C, dk = q_ref.shape
    lg = lg_ref[0:1, :dk]

    @pl.when(pl.program_id(2) == 0)
    def _():
        state_ref[...] = jnp.zeros_like(state_ref)
        row = _row_index((C, dk))
        qd_ref[...] = jnp.exp(lg * (row + 1.0))
        kd_ref[...] = jnp.exp(lg * (C - 1.0 - row))
        ii = lax.broadcasted_iota(jnp.int32, (C, C), 0)
        jj = lax.broadcasted_iota(jnp.int32, (C, C), 1)
        intra_ref[...] = jnp.exp(lg_ref[0:1, :C] * jnp.abs(ii - jj).astype(F32))

    q, k = _ret_qkv(q_ref, k_ref, cos_ref, sin_ref)
    vb = v_ref[...].astype(BF16)
    q_in = (q * qd_ref[...]).astype(BF16)
    k_out = (k * kd_ref[...]).astype(BF16)
    s = lax.dot_general(q.astype(BF16), k.astype(BF16), (((1,), (1,)), ((), ())),
                        preferred_element_type=F32) * intra_ref[...]
    state = state_ref[...]
    y = jnp.dot(s.astype(BF16), vb, preferred_element_type=F32)
    y = y + jnp.dot(q_in, state.astype(BF16), preferred_element_type=F32)
    y_ref[...] = y
    kv = lax.dot_general(k_out, vb, (((0,), (0,)), ((), ())), preferred_element_type=F32)
    state_ref[...] = state * jnp.exp(lg * float(C)) + kv


def _ret_bwd_body(q_ref, k_ref, v_ref, g_ref, cos_ref, sin_ref, lg_ref, yf_ref, y_ref,
                  state_ref, qd_ref, kd_ref):
    C, dk = q_ref.shape
    lg = lg_ref[0:1, :dk]

    @pl.when(pl.program_id(2) == 0)
    def _():
        state_ref[...] = jnp.zeros_like(state_ref)
        row = _row_index((C, dk))
        qd_ref[...] = jnp.exp(lg * (C - row))
        kd_ref[...] = jnp.exp(lg * row)

    q, k = _ret_qkv(q_ref, k_ref, cos_ref, sin_ref)
    vb = v_ref[...].astype(BF16)
    q_in = (q * qd_ref[...]).astype(BF16)
    k_out = (k * kd_ref[...]).astype(BF16)
    state = state_ref[...]
    y = yf_ref[...] + jnp.dot(q_in, state.astype(BF16), preferred_element_type=F32)
    kv = lax.dot_general(k_out, vb, (((0,), (0,)), ((), ())), preferred_element_type=F32)
    state_ref[...] = state * jnp.exp(lg * float(C)) + kv
    mu = jnp.mean(y, axis=-1, keepdims=True)
    d = y - mu
    var = jnp.mean(d * d, axis=-1, keepdims=True)
    y_ref[...] = _silu(g_ref[...]) * (d * lax.rsqrt(var + 1e-5))


def _retention(p3, cos, sin, lg_tab, H):
    B, S, _ = p3.shape
    dk = RET_HEAD_DIM
    C = min(512, S)
    nc = S // C
    W = lg_tab.shape[-1]
    blk = lambda off, rev: pl.BlockSpec(
        (None, C, dk),
        (lambda b, h, c: (b, nc - 1 - c, off + h)) if rev else (lambda b, h, c: (b, c, off + h)))
    tab = lambda rev: pl.BlockSpec(
        (C, dk // 2), (lambda b, h, c: (nc - 1 - c, 0)) if rev else (lambda b, h, c: (c, 0)))
    lg_spec = pl.BlockSpec((None, 8, W), lambda b, h, c: (h, 0, 0))
    vmem = 32 << 20
    y_f = pl.pallas_call(
        _ret_fwd_body,
        grid=(B, H, nc),
        in_specs=[blk(0, False), blk(H, False), blk(2 * H, False), tab(False), tab(False), lg_spec],
        out_specs=pl.BlockSpec((None, C, dk), lambda b, h, c: (b, c, h)),
        out_shape=jax.ShapeDtypeStruct((B, S, H * dk), F32),
        scratch_shapes=[pltpu.VMEM((dk, dk), F32), pltpu.VMEM((C, dk), F32), pltpu.VMEM((C, dk), F32),
                        pltpu.VMEM((C, C), F32)],
        compiler_params=_params(("parallel", "parallel", "arbitrary"), vmem),
        name="ret_fwd",
    )(p3, p3, p3, cos, sin, lg_tab)
    return pl.pallas_call(
        _ret_bwd_body,
        grid=(B, H, nc),
        in_specs=[blk(0, True), blk(H, True), blk(2 * H, True), blk(3 * H, True),
                  tab(True), tab(True), lg_spec,
                  pl.BlockSpec((None, C, dk), lambda b, h, c: (b, nc - 1 - c, h))],
        out_specs=pl.BlockSpec((None, C, dk), lambda b, h, c: (b, nc - 1 - c, h)),
        out_shape=jax.ShapeDtypeStruct((B, S, H * dk), F32),
        scratch_shapes=[pltpu.VMEM((dk, dk), F32), pltpu.VMEM((C, dk), F32), pltpu.VMEM((C, dk), F32)],
        compiler_params=_params(("parallel", "parallel", "arbitrary"), vmem),
        name="ret_bwd",
    )(p3, p3, p3, p3, cos, sin, lg_tab, y_f)


HALO = 8


def _fill_haloed(xs_ref, x_ref, S):
    zeros = jnp.zeros((HALO, xs_ref.shape[1]), F32)
    xs_ref[0:HALO, :] = zeros
    xs_ref[S + HALO:S + 2 * HALO, :] = zeros
    xs_ref[HALO:S + HALO, :] = x_ref[...]


def _conv_rows(xs_ref, r0, R, taps, left, bias):
    ext = xs_ref[pl.ds(r0, R + 2 * HALO), :]
    n = R + 2 * HALO
    acc = None
    for j, w in enumerate(taps):
        d = j - left
        sh = ext if d == 0 else pltpu.roll(ext, (-d) % n, axis=0)
        term = w * sh[HALO:HALO + R, :]
        acc = term if acc is None else acc + term
    return acc + bias


FFT_N2 = 128
TN2 = 32
NT = FFT_N2 // TN2
STAGE_CB = LANES


def _store_tiled(o_ref, n1, block):
    for h in range(NT):
        o_ref[h, pl.ds(pl.multiple_of(n1 * TN2, TN2), TN2), :] = block[h * TN2:(h + 1) * TN2, :]


def _hy_conv_body(u_ref, w_ref, b_ref, o_ref, xs_ref, *, S, R, K):
    _fill_haloed(xs_ref, u_ref, S)
    taps = [w_ref[j:j + 1, :] for j in range(K)]
    bias = b_ref[...]
    per = R // FFT_N2

    def body(i, carry):
        r0 = pl.multiple_of(i * R, R)
        res = _conv_rows(xs_ref, r0, R, taps, (K - 1) // 2, bias)
        for q in range(per):
            _store_tiled(o_ref, i * per + q, res[q * FFT_N2:(q + 1) * FFT_N2, :])
        return carry

    lax.fori_loop(0, S // R, body, 0)


def _hy_conv(p3, col0, width, w, b):
    B, S, _ = p3.shape
    K = w.shape[0]
    cb = LANES
    nb = width // cb
    C = width // 3
    R = min(256, S)
    wp = jnp.zeros((8, width), F32).at[:K].set(w)
    off = col0 // cb
    per = C // cb
    return pl.pallas_call(
        functools.partial(_hy_conv_body, S=S, R=R, K=K),
        grid=(B, nb),
        in_specs=[
            pl.BlockSpec((None, S, cb), lambda b_, j: (b_, 0, off + j)),
            pl.BlockSpec((8, cb), lambda b_, j: (0, j)),
            pl.BlockSpec((1, cb), lambda b_, j: (0, j)),
        ],
        out_specs=pl.BlockSpec((None, NT, S // NT, cb), lambda b_, j: ((j // per) * B + b_, 0, 0, j % per)),
        out_shape=jax.ShapeDtypeStruct((3 * B, NT, S // NT, C), F32),
        scratch_shapes=[pltpu.VMEM((S + 2 * HALO, cb), F32)],
        compiler_params=_params(("parallel", "parallel"), 6 * S * cb * 4 + (8 << 20)),
        name="hy_conv",
    )(p3, wp, b.reshape(1, width))


def _hy_filter_body(z_ref, w1_ref, b1_ref, f1_ref, w2_ref, b2_ref, f2_ref, w3_ref, dec_ref,
                    o_ref, sum_ref, *, L, tr):
    i = pl.program_id(0)
    hp = lax.Precision.HIGHEST
    z = z_ref[...]
    h = jnp.sin(f1_ref[...] * (jnp.dot(z, w1_ref[...], precision=hp, preferred_element_type=F32) + b1_ref[...]))
    h = jnp.sin(f2_ref[...] * (jnp.dot(h, w2_ref[...], precision=hp, preferred_element_type=F32) + b2_ref[...]))
    h = jnp.dot(h, w3_ref[...], precision=hp, preferred_element_type=F32)
    t = z[:, 0:1]
    h = h * jnp.exp(-t * jnp.abs(dec_ref[...]))
    m = i * tr + lax.broadcasted_iota(jnp.int32, h.shape, 0)
    filt = jnp.where(m == L, 0.0, h)
    for q in range(tr // FFT_N2):
        blk = filt[q * FFT_N2:(q + 1) * FFT_N2, :]
        for hh in range(NT):
            o_ref[hh, q * TN2:(q + 1) * TN2, :] = blk[hh * TN2:(hh + 1) * TN2, :]

    @pl.when(i == 0)
    def _():
        sum_ref[...] = jnp.zeros_like(sum_ref)

    sum_ref[...] += jnp.sum(jnp.abs(filt), axis=0, keepdims=True)


def _hy_filter(zfeat, w1, b1, f1, w2, b2, f2, w3, dec, L):
    N, E = zfeat.shape
    hid = w1.shape[1]
    C4 = w3.shape[1]
    tr = min(256, L)
    nt_half = L // tr
    full = lambda a: pl.BlockSpec(a.shape, lambda i: (0,) * a.ndim)
    side = lambda a: pl.BlockSpec((a.shape[0], C4 // 2), lambda i: (0, i // nt_half))
    args = (w1, b1, f1, w2, b2, f2, w3, dec)
    return pl.pallas_call(
        functools.partial(_hy_filter_body, L=L, tr=tr),
        grid=(N // tr,),
        in_specs=[pl.BlockSpec((tr, E), lambda i: (i, 0))] + [full(a) for a in args[:6]]
        + [side(w3), side(dec)],
        out_specs=[pl.BlockSpec((NT, tr // NT, C4 // 2), lambda i: (0, i, 0)),
                   pl.BlockSpec((1, C4 // 2), lambda i: (0, 0))],
        out_shape=[jax.ShapeDtypeStruct((NT, N // NT, C4 // 2), F32),
                   jax.ShapeDtypeStruct((1, C4 // 2), F32)],
        compiler_params=_params(("arbitrary",), 32 << 20),
        name="hy_filter",
    )(zfeat, *args)


def _stage_a_body(w_ref, x_ref, o_ref, *, nparts, rows):
    mo = w_ref.shape[0]
    w = w_ref[...]

    def body(r, carry):
        parts = [x_ref[p, pl.ds(r, rows, stride=TN2), :] for p in range(nparts)]
        x = parts[0] if nparts == 1 else jnp.concatenate(parts, axis=0)
        o_ref[pl.ds(r, mo, stride=TN2), :] = jnp.dot(w, x.astype(BF16), preferred_element_type=F32)
        return carry

    lax.fori_loop(0, TN2, body, 0, unroll=8)


def _stage_a(wmat, x, group, nparts):
    mo, K = wmat.shape
    _, _, T, C = x.shape
    rows = T // TN2
    assert K == nparts * rows
    cb = min(STAGE_CB, C)
    vmem = 2 * nparts * T * cb * 4 + 2 * mo * TN2 * cb * 4 + (12 << 20)
    return pl.pallas_call(
        functools.partial(_stage_a_body, nparts=nparts, rows=rows),
        grid=(C // cb, NT),
        in_specs=[pl.BlockSpec((mo, K), lambda j, h: (0, 0)),
                  pl.BlockSpec((nparts, None, T, cb), lambda j, h: (group, h, 0, j))],
        out_specs=pl.BlockSpec((None, mo * TN2, cb), lambda j, h: (h, 0, j)),
        out_shape=jax.ShapeDtypeStruct((NT, mo * TN2, C), F32),
        compiler_params=_params(("parallel", "parallel"), vmem),
        name="hy_stage_a",
    )(wmat, x)


def _hy_freq_body(g_ref, gi_ref, x_ref, f_ref, s_ref, o_ref, *, kb):
    n2 = FFT_N2
    inv = 1.0 / (s_ref[...] + 1e-6)

    def gather(ref, t):
        return jnp.concatenate([ref[h, ri, t] for ri in range(2) for h in range(NT)], axis=0)

    for t in range(kb):
        g = g_ref[t]
        y = jnp.dot(g, gather(x_ref, t).astype(BF16), preferred_element_type=F32)
        h = jnp.dot(g, gather(f_ref, t).astype(BF16), preferred_element_type=F32) * inv
        yr, yi = y[:n2], y[n2:]
        hr, hi = h[:n2], h[n2:]
        z = jnp.concatenate([yr * hr - yi * hi, yr * hi + yi * hr], axis=0).astype(BF16)
        e = jnp.dot(gi_ref[t], z, preferred_element_type=F32)
        for ri in range(2):
            for hh in range(NT):
                o_ref[hh, ri, t] = e[ri * n2 + hh * TN2:ri * n2 + (hh + 1) * TN2]


def _hy_freq(gmat, gimat, xa, fa, asum, order, C):
    N1 = xa.shape[2]
    n2 = FFT_N2
    kb = min(4, N1)
    blk = pl.BlockSpec((NT, 2, kb, TN2, C), lambda i: (0, 0, i, 0, 0))
    gspec = pl.BlockSpec((kb, 2 * n2, 2 * n2), lambda i: (i, 0, 0))
    return pl.pallas_call(
        functools.partial(_hy_freq_body, kb=kb),
        grid=(N1 // kb,),
        in_specs=[gspec, gspec, blk,
                  pl.BlockSpec((NT, 2, kb, TN2, C), lambda i: (0, 0, i, 0, order)),
                  pl.BlockSpec((1, C), lambda i: (0, order))],
        out_specs=blk,
        out_shape=jax.ShapeDtypeStruct(xa.shape, F32),
        compiler_params=_params(("parallel",), 6 * 2 * kb * n2 * C * 4 + 24 * n2 * C * 4 + (8 << 20)),
        name="hy_freq",
    )(gmat, gimat, xa, fa, asum)


def _stage_d_body(w_ref, e_ref, z_ref, x_ref, b_ref, o_ref, *, B, nh):
    w = w_ref[...]
    bias = b_ref[...]
    mi = w.shape[1]

    def body(r, carry):
        e = e_ref[pl.ds(r, mi, stride=TN2), :].astype(BF16)
        conv = jnp.dot(w, e, preferred_element_type=F32)
        rows = pl.ds(r, nh, stride=TN2)
        for b in range(B):
            o_ref[b, rows, :] = x_ref[b, rows, :] * (conv[b * nh:(b + 1) * nh] + bias * z_ref[b, rows, :])
        return carry

    lax.fori_loop(0, TN2, body, 0, unroll=8)


def _stage_d(w4, e, z, z_group, gate, gate_group, bias, B):
    mo, mi = w4.shape
    nh = mo // B
    T, C = z.shape[2], z.shape[3]
    cb = min(STAGE_CB, C)
    nat = lambda grp: pl.BlockSpec((B, None, T, cb), lambda j, h: (grp, h, 0, j))
    vmem = 2 * mi * TN2 * cb * 4 + 6 * B * T * cb * 4 + (12 << 20)
    return pl.pallas_call(
        functools.partial(_stage_d_body, B=B, nh=nh),
        grid=(C // cb, NT),
        in_specs=[pl.BlockSpec((mo, mi), lambda j, h: (0, 0)),
                  pl.BlockSpec((None, mi * TN2, cb), lambda j, h: (h, 0, j)),
                  nat(z_group), nat(gate_group),
                  pl.BlockSpec((1, cb), lambda j, h: (0, j))],
        out_specs=nat(0),
        out_shape=jax.ShapeDtypeStruct((B, NT, T, C), F32),
        compiler_params=_params(("parallel", "parallel"), vmem),
        name="hy_stage_d",
    )(w4, e, z, gate, bias)


def _dft_tables(S):
    N = 2 * S
    n2 = FFT_N2
    N1 = N // n2
    nh = S // n2
    k1 = np.arange(N1)
    f1 = np.exp(-2j * np.pi * np.outer(k1, np.arange(N1)) / N1)
    fa = f1[:, :nh]
    wa = np.block([[fa.real, -fa.imag], [fa.imag, fa.real]])
    wa_real = np.concatenate([f1.real, f1.imag], axis=0)
    a2 = np.arange(n2)
    base = np.exp(-2j * np.pi * np.outer(a2, a2) / n2)
    tw = np.exp(-2j * np.pi * np.outer(k1, a2) / N)
    g = base[None, :, :] * tw[:, None, :]
    gi = np.conj(np.transpose(g, (0, 2, 1)))
    blockify = lambda c: np.concatenate(
        [np.concatenate([c.real, -c.imag], axis=-1), np.concatenate([c.imag, c.real], axis=-1)], axis=-2)
    f4 = np.exp(2j * np.pi * np.outer(np.arange(nh), k1) / N1) / N
    wd = np.block([[f4.real, -f4.imag], [f4.imag, f4.real]])
    cast = lambda a: jnp.asarray(a, dtype=F32).astype(BF16)
    return dict(N1=N1, nh=nh, wa=cast(wa), wa_real=cast(wa_real), g=cast(blockify(g)),
                gi=cast(blockify(gi)), wd=cast(wd))


def _hyena(p3, col0, C, tabs, conv_w, conv_b, filt_params, hy_bias, zfeat):
    B, S, _ = p3.shape
    N1 = tabs["N1"]
    u = _hy_conv(p3, col0, 3 * C, conv_w, conv_b)
    filt, asum = _hy_filter(zfeat, *filt_params, S)
    fa = _stage_a(tabs["wa_real"], filt[None], 0, 1).reshape(NT, 2, N1, TN2, 2 * C)
    z = u
    for order in range(2):
        xa = _stage_a(tabs["wa"], z, 0, B)
        e = _hy_freq(tabs["g"], tabs["gi"], xa.reshape(NT, 2, N1, TN2, C), fa, asum, order, C)
        z = _stage_d(tabs["wd"], e.reshape(NT, 2 * N1 * TN2, C), z, 0, u, 1 + order,
                     hy_bias[order].reshape(1, C), B)
    return z


def _scan_rows(a, b, reverse):
    R = a.shape[0]
    row = lax.broadcasted_iota(jnp.int32, a.shape, 0)
    d = 1
    while d < R:
        if reverse:
            a_s = pltpu.roll(a, R - d, axis=0)
            b_s = pltpu.roll(b, R - d, axis=0)
            m = row < R - d
        else:
            a_s = pltpu.roll(a, d, axis=0)
            b_s = pltpu.roll(b, d, axis=0)
            m = row >= d
        b = jnp.where(m, a * b_s + b, b)
        a = jnp.where(m, a * a_s, a)
        d *= 2
    return a, b


def _gelu_tanh(x):
    return 0.5 * x * (1.0 + jnp.tanh(0.7978845608028654 * (x + 0.044715 * x * x * x)))


def _lru_body(g_ref, x_ref, cw_ref, cb_ref, w_ref, bias_ref, lam_ref, o_ref,
              xs_ref, hf_ref, ab_ref, bb_ref, *, S, R, K):
    bd = x_ref.shape[1]
    _fill_haloed(xs_ref, x_ref, S)
    taps = [cw_ref[j:j + 1, :] for j in range(K)]
    cbias = cb_ref[...]
    w = w_ref[...]
    bias = bias_ref[...]
    nl = -lam_ref[...]
    sp = jnp.maximum(nl, 0.0) + jnp.log1p(jnp.exp(-jnp.abs(nl)))
    sp0 = sp[0:1, :]
    sp1 = sp[1:2, :]
    nchunk = S // R

    def gates(r, i, spd, xr):
        log_a = -LRU_C * jax.nn.sigmoid(r) * spd
        a = jnp.exp(log_a)
        b = jnp.sqrt(1.0 - a * a) * (jax.nn.sigmoid(i) * xr)
        return a, b

    def fwd(i, hc):
        r0 = pl.multiple_of(i * R, R)
        xr = _conv_rows(xs_ref, r0, R, taps, (K - 1) // 2, cbias)
        proj = jnp.dot(xr.astype(BF16), w, preferred_element_type=F32) + bias
        a0, b0 = gates(proj[:, 0:bd], proj[:, bd:2 * bd], sp0, xr)
        a1, b1 = gates(proj[:, 2 * bd:3 * bd], proj[:, 3 * bd:4 * bd], sp1, xr)
        ab_ref[pl.ds(r0, R), :] = a1
        bb_ref[pl.ds(r0, R), :] = b1
        ac, hl = _scan_rows(a0, b0, reverse=False)
        h = ac * hc + hl
        hf_ref[pl.ds(r0, R), :] = h
        return h[R - 1:R, :]

    lax.fori_loop(0, nchunk, fwd, jnp.zeros((1, bd), F32))

    def bwd(t, hc):
        r0 = pl.multiple_of((nchunk - 1 - t) * R, R)
        sl = pl.ds(r0, R)
        ac, hl = _scan_rows(ab_ref[sl, :], bb_ref[sl, :], reverse=True)
        h = ac * hc + hl
        o_ref[sl, :] = _gelu_tanh(g_ref[sl, :]) * (hf_ref[sl, :] + h)
        return h[0:1, :]

    lax.fori_loop(0, nchunk, bwd, jnp.zeros((1, bd), F32))


def _lru(p3, col_g, col_x, width, conv_w, conv_b, wa, ba, wx, bx, lam):
    B, S, _ = p3.shape
    bd = LRU_BLOCK_DIM
    nb = width // bd
    K = conv_w.shape[0]
    R = min(256, S)
    cw = jnp.zeros((8, width), F32).at[:K].set(conv_w)
    wcat = jnp.concatenate([wa[0], wx[0], wa[1], wx[1]], axis=-1).astype(BF16)
    bsplit = lambda v: v.reshape(nb, 1, bd)
    bcat = jnp.concatenate([bsplit(ba[0]), bsplit(bx[0]), bsplit(ba[1]), bsplit(bx[1])], axis=-1)
    lam8 = jnp.zeros((8, width), F32).at[:2].set(lam)
    og, ox = col_g // bd, col_x // bd
    return pl.pallas_call(
        functools.partial(_lru_body, S=S, R=R, K=K),
        grid=(B, nb),
        in_specs=[
            pl.BlockSpec((None, S, bd), lambda b, j: (b, 0, og + j)),
            pl.BlockSpec((None, S, bd), lambda b, j: (b, 0, ox + j)),
            pl.BlockSpec((8, bd), lambda b, j: (0, j)),
            pl.BlockSpec((1, bd), lambda b, j: (0, j)),
            pl.BlockSpec((None, bd, 4 * bd), lambda b, j: (j, 0, 0)),
            pl.BlockSpec((None, 1, 4 * bd), lambda b, j: (j, 0, 0)),
            pl.BlockSpec((8, bd), lambda b, j: (0, j)),
        ],
        out_specs=pl.BlockSpec((None, S, bd), lambda b, j: (b, 0, j)),
        out_shape=jax.ShapeDtypeStruct((B, S, width), F32),
        scratch_shapes=[pltpu.VMEM((S + 2 * HALO, bd), F32), pltpu.VMEM((S, bd), F32),
                        pltpu.VMEM((S, bd), F32), pltpu.VMEM((S, bd), F32)],
        compiler_params=_params(("parallel", "parallel"), 10 * S * bd * 4 + (12 << 20)),
        name="lru",
    )(p3, p3, cw, conv_b.reshape(1, width), wcat, bcat, lam8)


def _mixnorm_body(a_ref, b_ref, c_ref, g_ref, o_ref):
    off = 0
    for ref in (a_ref, b_ref, c_ref):
        if len(ref.shape) == 3:
            y = jnp.concatenate([ref[h] for h in range(NT)], axis=0)
        else:
            y = ref[...]
        w = y.shape[1]
        ms = jnp.mean(y * y, axis=-1, keepdims=True)
        o_ref[:, off:off + w] = (y * lax.rsqrt(ms + 1e-6) * g_ref[:, off:off + w]).astype(BF16)
        off += w


def _mixnorm(ya, yb_tiled, yc, gain):
    M = ya.shape[0]
    B, _, T, wb = yb_tiled.shape
    nh = T // TN2
    widths = (ya.shape[1], wb, yc.shape[1])
    D = sum(widths)
    tm = FFT_N2
    nat = lambda w: pl.BlockSpec((tm, w), lambda i: (i, 0))
    tiled = pl.BlockSpec((None, NT, None, TN2, wb), lambda i: (i // nh, 0, i % nh, 0, 0))
    return pl.pallas_call(
        _mixnorm_body,
        grid=(M // tm,),
        in_specs=[nat(widths[0]), tiled, nat(widths[2]), pl.BlockSpec((1, D), lambda i: (0, 0))],
        out_specs=pl.BlockSpec((tm, D), lambda i: (i, 0)),
        out_shape=jax.ShapeDtypeStruct((M, D), BF16),
        compiler_params=_params(("parallel",), 32 << 20),
        name="mixnorm",
    )(ya, yb_tiled.reshape(B, NT, nh, TN2, wb), yc, gain)


def kernel(x, ffa_w_gate, ffa_w_up, ffa_w_down, ffb_w_gate, ffb_w_up, ffb_w_down, ln_gain, ln_bias,
           w_in, w_out, mix_norm_gain, hy_conv_w, hy_conv_b, hy_filt_w1, hy_filt_b1, hy_filt_freq1,
           hy_filt_w2, hy_filt_b2, hy_filt_freq2, hy_filt_w3, hy_decay, hy_bias, lru_conv_w,
           lru_conv_b, lru_wa, lru_ba, lru_wx, lru_bx, lru_lambda):
    B, S, D = x.shape
    depth = w_in.shape[0]
    M = B * S
    alpha = (2 * depth) ** 0.25
    R = D // 2
    H = R // RET_HEAD_DIM
    Hy = D // 4
    Lw = D // 4
    col_hy = 4 * R
    col_lg = col_hy + 3 * Hy
    col_lx = col_lg + Lw

    pos = jnp.arange(S, dtype=F32)
    half = RET_HEAD_DIM // 2
    inv_freq = 1.0 / (10000.0 ** jnp.linspace(0.0, 1.0, half, dtype=F32))
    ang = pos[:, None] * inv_freq[None, :]
    cos, sin = jnp.cos(ang), jnp.sin(ang)
    log_g = jnp.log1p(-jnp.exp2(-5.0 - jnp.arange(H, dtype=F32)))
    lg_tab = jnp.broadcast_to(log_g[:, None, None], (H, 8, max(512, RET_HEAD_DIM)))
    tabs = _dft_tables(S)
    m = jnp.arange(2 * S)
    fpos = jnp.where(m < S, m, 2 * S - m).astype(F32)
    emb = hy_filt_w1.shape[1]
    bands = (emb - 1) // 2
    fr = jnp.linspace(1e-4, bands - 1, bands, dtype=F32)
    wang = 2.0 * math.pi * fpos / S
    zfeat = jnp.concatenate([(fpos / max(S - 1, 1))[:, None], jnp.cos(wang[:, None] * fr),
                             -jnp.sin(wang[:, None] * fr)], axis=-1)
    epad = LANES
    zfeat = jnp.pad(zfeat, ((0, 0), (0, epad - emb)))

    assert B == 2, "the Hyena DFT carries the two batch rows as one complex signal"
    ffa = (_to_bf16(ffa_w_gate), _to_bf16(ffa_w_up), _to_bf16(ffa_w_down))
    ffb = (_to_bf16(ffb_w_gate), _to_bf16(ffb_w_up), _to_bf16(ffb_w_down))
    w_in_b, w_out_b = _to_bf16(w_in), _to_bf16(w_out)
    row = lambda v: v.reshape(1, -1)

    h = x.reshape(M, D)
    for l in range(depth):
        h = _ffn(h, *ffa, l, row(ln_gain[l, 0]), row(ln_bias[l, 0]), alpha)
        p3 = _inproj(h, w_in_b, l).reshape(B, S, -1)
        y_ret = _retention(p3, cos, sin, lg_tab, H)
        w1p = jnp.pad(hy_filt_w1[l], ((0, epad - emb), (0, 0)))
        filt_params = (w1p, row(hy_filt_b1[l]), row(hy_filt_freq1[l]), hy_filt_w2[l],
                       row(hy_filt_b2[l]), row(hy_filt_freq2[l]), hy_filt_w3[l], row(hy_decay[l]))
        y_hy = _hyena(p3, col_hy, Hy, tabs, hy_conv_w[l], hy_conv_b[l], filt_params, hy_bias[l], zfeat)
        y_lru = _lru(p3, col_lg, col_lx, Lw, lru_conv_w[l], lru_conv_b[l], lru_wa[l], lru_ba[l],
                     lru_wx[l], lru_bx[l], lru_lambda[l])
        yb = _mixnorm(y_ret.reshape(M, R), y_hy, y_lru.reshape(M, Lw),
                      row(mix_norm_gain[l]))
        h = _outproj(yb, w_out_b, l, h, row(ln_gain[l, 1]), row(ln_bias[l, 1]), alpha)
        h = _ffn(h, *ffb, l, row(ln_gain[l, 2]), row(ln_bias[l, 2]), alpha)
    return h.reshape(B, S, D)
```

```python
import functools
import math

import numpy as np
import jax
import jax.numpy as jnp
from jax import lax
from jax.experimental import pallas as pl
from jax.experimental.pallas import tpu as pltpu

F32 = jnp.float32
BF16 = jnp.bfloat16

LANES = 128
RET_HEAD_DIM = 256
LRU_BLOCK_DIM = 128
LRU_C = 8.0
VMEM_CAP = 60 * 1024 * 1024


def _params(semantics, vmem_bytes):
    return pltpu.CompilerParams(
        dimension_semantics=semantics,
        vmem_limit_bytes=int(min(VMEM_CAP, max(vmem_bytes, 16 * 1024 * 1024))))


def _layer_norm_rows(y, g, b, eps=1e-5):
    mu = jnp.mean(y, axis=-1, keepdims=True)
    d = y - mu
    var = jnp.mean(d * d, axis=-1, keepdims=True)
    return d * lax.rsqrt(var + eps) * g + b


def _silu(x):
    return x * jax.nn.sigmoid(x)


def _ffn_body(x_ref, wgu_a, wgu_b, wd_a, wd_b, g_ref, b_ref, o_ref, xb_ref, *, alpha, rows, slab, nf):
    j = pl.program_id(1)
    nj = pl.num_programs(1)
    tm = x_ref.shape[0]
    tf = wd_a.shape[0]
    D = o_ref.shape[1]

    @pl.when(j == 0)
    def _():
        xb_ref[...] = x_ref[...].astype(BF16)
        o_ref[...] = jnp.zeros_like(o_ref)

    def hidden(wgu_ref):
        gu = jnp.dot(xb_ref[...], wgu_ref[...], preferred_element_type=F32)
        return (_silu(gu[:, :tf]) * gu[:, tf:]).astype(BF16)

    def accumulate(pairs):
        hs = [(hidden(wgu_ref), wd_ref) for wgu_ref, wd_ref in pairs]
        for c0 in range(0, D, slab):
            part = None
            for h, wd_ref in hs:
                p = jnp.dot(h, wd_ref[:, c0:c0 + slab], preferred_element_type=F32)
                part = p if part is None else part + p
            o_ref[:, c0:c0 + slab] += part

    if nf % 2 == 0:
        accumulate([(wgu_a, wd_a), (wgu_b, wd_b)])
    else:
        @pl.when(2 * j + 1 < nf)
        def _():
            accumulate([(wgu_a, wd_a), (wgu_b, wd_b)])

        @pl.when(2 * j + 1 >= nf)
        def _():
            accumulate([(wgu_a, wd_a)])

    @pl.when(j == nj - 1)
    def _():
        g = g_ref[...]
        b = b_ref[...]

        def body(r, carry):
            sl = pl.ds(pl.multiple_of(r * rows, rows), rows)
            y = alpha * x_ref[sl, :] + 0.5 * o_ref[sl, :]
            o_ref[sl, :] = _layer_norm_rows(y, g, b)
            return carry

        lax.fori_loop(0, tm // rows, body, 0)


def _cast_body(x_ref, o_ref):
    o_ref[...] = x_ref[...].astype(BF16)


def _to_bf16(w):
    L, R, C = w.shape
    tr = 256
    while tr > 8 and (R % tr or tr * C * 4 > (12 << 20)):
        tr //= 2
    spec = pl.BlockSpec((None, tr, C), lambda l, i: (l, i, 0))
    return pl.pallas_call(
        _cast_body,
        grid=(L, R // tr),
        in_specs=[spec],
        out_specs=spec,
        out_shape=jax.ShapeDtypeStruct(w.shape, BF16),
        compiler_params=_params(("parallel", "parallel"), 3 * tr * C * 4 + (8 << 20)),
        name="to_bf16",
    )(w)


FFN_TF = 256


def _pack_gate_up_body(g_ref, u_ref, o_ref, *, tf):
    for c in range(g_ref.shape[1] // tf):
        o_ref[c, :, :tf] = g_ref[:, c * tf:(c + 1) * tf].astype(BF16)
        o_ref[c, :, tf:] = u_ref[:, c * tf:(c + 1) * tf].astype(BF16)


def _pack_gate_up(wg, wu):
    L, D, F = wg.shape
    tf = FFN_TF if F % FFN_TF == 0 else F
    tr = min(128, D)
    spec = pl.BlockSpec((None, tr, F), lambda l, i: (l, i, 0))
    return pl.pallas_call(
        functools.partial(_pack_gate_up_body, tf=tf),
        grid=(L, D // tr),
        in_specs=[spec, spec],
        out_specs=pl.BlockSpec((None, F // tf, tr, 2 * tf), lambda l, i: (l, 0, i, 0)),
        out_shape=jax.ShapeDtypeStruct((L, F // tf, D, 2 * tf), BF16),
        compiler_params=_params(("parallel", "parallel"), 6 * tr * F * 4 + (8 << 20)),
        name="pack_gate_up",
    )(wg, wu)


def _ffn(x, wgu, wd, layer, g, b, alpha):
    M, D = x.shape
    F = wd.shape[1]
    tm = min(512, M)
    tf = FFN_TF if F % FFN_TF == 0 else F
    nf = F // tf
    rows = min(16, tm)
    slab = 1024 if D % 1024 == 0 else D
    vmem = (tm * D * 4 + tm * D * 2 + 2 * tm * D * 4 + 2 * 2 * 3 * D * tf * 2 + 2 * 6 * tm * tf * 4
            + 2 * tm * slab * 4)
    blk_a = lambda i, j: 2 * j
    blk_b = lambda i, j: jnp.minimum(2 * j + 1, nf - 1)
    wgu_spec = lambda blk: pl.BlockSpec((None, None, D, 2 * tf), lambda i, j: (layer, blk(i, j), 0, 0))
    wd_spec = lambda blk: pl.BlockSpec((None, tf, D), lambda i, j: (layer, blk(i, j), 0))
    return pl.pallas_call(
        functools.partial(_ffn_body, alpha=alpha, rows=rows, slab=slab, nf=nf),
        grid=(M // tm, (nf + 1) // 2),
        in_specs=[
            pl.BlockSpec((tm, D), lambda i, j: (i, 0), pipeline_mode=pl.Buffered(1)),
            wgu_spec(blk_a), wgu_spec(blk_b), wd_spec(blk_a), wd_spec(blk_b),
            pl.BlockSpec((1, D), lambda i, j: (0, 0)),
            pl.BlockSpec((1, D), lambda i, j: (0, 0)),
        ],
        out_specs=pl.BlockSpec((tm, D), lambda i, j: (i, 0)),
        out_shape=jax.ShapeDtypeStruct((M, D), F32),
        scratch_shapes=[pltpu.VMEM((tm, D), BF16)],
        compiler_params=_params(("parallel", "arbitrary"), vmem + (2 << 20)),
        name="ffn",
    )(x, wgu, wgu, wd, wd, g, b)


def _inproj_body(x_ref, w_ref, o_ref, xb_ref):
    @pl.when(pl.program_id(1) == 0)
    def _():
        xb_ref[...] = x_ref[...].astype(BF16)

    o_ref[...] = jnp.dot(xb_ref[...], w_ref[...], preferred_element_type=F32)


def _inproj(x, w, layer):
    M, K = x.shape
    N = w.shape[2]
    tm = min(512, M)
    tn = 1024 if N % 1024 == 0 else (512 if N % 512 == 0 else 128)
    vmem = 2 * tm * K * 4 + tm * K * 2 + 2 * K * tn * 2 + 2 * tm * tn * 4
    return pl.pallas_call(
        _inproj_body,
        grid=(M // tm, N // tn),
        in_specs=[
            pl.BlockSpec((tm, K), lambda i, j: (i, 0)),
            pl.BlockSpec((None, K, tn), lambda i, j: (layer, 0, j)),
        ],
        out_specs=pl.BlockSpec((tm, tn), lambda i, j: (i, j)),
        out_shape=jax.ShapeDtypeStruct((M, N), F32),
        scratch_shapes=[pltpu.VMEM((tm, K), BF16)],
        compiler_params=_params(("parallel", "arbitrary"), vmem + (4 << 20)),
        name="inproj",
    )(x, w)


def _outproj_body(y_ref, w_ref, x_ref, g_ref, b_ref, o_ref, *, alpha, rows, nj, slab):
    j = pl.program_id(1)
    tm, D = o_ref.shape
    rc = x_ref.shape[0]

    @pl.when(j == 0)
    def _():
        o_ref[...] = jnp.zeros_like(o_ref)

    yk = y_ref[...]
    for c0 in range(0, D, slab):
        o_ref[:, c0:c0 + slab] += jnp.dot(yk, w_ref[:, c0:c0 + slab], preferred_element_type=F32)
    sl = pl.ds(pl.multiple_of(j * rc, rc), rc)
    o_ref[sl, :] += alpha * x_ref[...]

    @pl.when(j == nj - 1)
    def _():
        g = g_ref[...]
        b = b_ref[...]

        def body(r, carry):
            rs = pl.ds(pl.multiple_of(r * rows, rows), rows)
            o_ref[rs, :] = _layer_norm_rows(o_ref[rs, :], g, b)
            return carry

        lax.fori_loop(0, tm // rows, body, 0)


def _outproj(y, w, layer, x, g, b, alpha):
    M, K = y.shape
    D = w.shape[2]
    tm = min(512, M)
    tk = 1024 if K % 1024 == 0 else K
    nj = K // tk
    rc = tm // nj
    rows = min(16, tm)
    slab = 1024 if D % 1024 == 0 else D
    vmem = 2 * tm * tk * 2 + 2 * tk * D * 2 + 2 * rc * D * 4 + 2 * tm * D * 4 + 2 * tm * slab * 4
    return pl.pallas_call(
        functools.partial(_outproj_body, alpha=alpha, rows=rows, nj=nj, slab=slab),
        grid=(M // tm, nj),
        in_specs=[
            pl.BlockSpec((tm, tk), lambda i, j: (i, j)),
            pl.BlockSpec((None, tk, D), lambda i, j: (layer, j, 0)),
            pl.BlockSpec((rc, D), lambda i, j: (i * nj + j, 0)),
            pl.BlockSpec((1, D), lambda i, j: (0, 0)),
            pl.BlockSpec((1, D), lambda i, j: (0, 0)),
        ],
        out_specs=pl.BlockSpec((tm, D), lambda i, j: (i, 0)),
        out_shape=jax.ShapeDtypeStruct((M, D), F32),
        compiler_params=_params(("parallel", "arbitrary"), vmem + (6 << 20)),
        name="outproj",
    )(y, w, x, g, b)


def _rotary(x, cos, sin):
    half = x.shape[-1] // 2
    x1 = x[:, :half]
    x2 = x[:, half:]
    return jnp.concatenate([x1 * cos - x2 * sin, x1 * sin + x2 * cos], axis=-1)


def _ret_qkv(q_ref, k_ref, cos_ref, sin_ref):
    cos = cos_ref[...]
    sin = sin_ref[...]
    q = _rotary(q_ref[...], cos, sin)
    k = _rotary(k_ref[...], cos, sin) * (RET_HEAD_DIM ** -0.5)
    return q, k


def _row_index(shape):
    return lax.broadcasted_iota(jnp.int32, shape, 0).astype(F32)


def _ret_fwd_body(q_ref, k_ref, v_ref, cos_ref, sin_ref, lg_ref, y_ref,
                  state_ref, qd_ref, kd_ref, intra_ref):
    C, dk = q_ref.shape
    lg = lg_ref[0:1, :dk]

    @pl.when(pl.program_id(2) == 0)
    def _():
        state_ref[...] = jnp.zeros_like(state_ref)
        row = _row_index((C, dk))
        qd_ref[...] = jnp.exp(lg * (row + 1.0))
        kd_ref[...] = jnp.exp(lg * (C - 1.0 - row))
        ii = lax.broadcasted_iota(jnp.int32, (C, C), 0)
        jj = lax.broadcasted_iota(jnp.int32, (C, C), 1)
        intra_ref[...] = jnp.exp(lg_ref[0:1, :C] * jnp.abs(ii - jj).astype(F32))

    q, k = _ret_qkv(q_ref, k_ref, cos_ref, sin_ref)
    vb = v_ref[...].astype(BF16)
    q_in = (q * qd_ref[...]).astype(BF16)
    k_out = (k * kd_ref[...]).astype(BF16)
    s = lax.dot_general(q.astype(BF16), k.astype(BF16), (((1,), (1,)), ((), ())),
                        preferred_element_type=F32) * intra_ref[...]
    state = state_ref[...]
    y = jnp.dot(s.astype(BF16), vb, preferred_element_type=F32)
    y = y + jnp.dot(q_in, state.astype(BF16), preferred_element_type=F32)
    y_ref[...] = y
    kv = lax.dot_general(k_out, vb, (((0,), (0,)), ((), ())), preferred_element_type=F32)
    state_ref[...] = state * jnp.exp(lg * float(C)) + kv


def _ret_bwd_body(q_ref, k_ref, v_ref, g_ref, cos_ref, sin_ref, lg_ref, yf_ref, y_ref,
                  state_ref, qd_ref, kd_ref):
    C, dk = q_ref.shape
    lg = lg_ref[0:1, :dk]

    @pl.when(pl.program_id(2) == 0)
    def _():
        state_ref[...] = jnp.zeros_like(state_ref)
        row = _row_index((C, dk))
        qd_ref[...] = jnp.exp(lg * (C - row))
        kd_ref[...] = jnp.exp(lg * row)

    q, k = _ret_qkv(q_ref, k_ref, cos_ref, sin_ref)
    vb = v_ref[...].astype(BF16)
    q_in = (q * qd_ref[...]).astype(BF16)
    k_out = (k * kd_ref[...]).astype(BF16)
    state = state_ref[...]
    y = yf_ref[...] + jnp.dot(q_in, state.astype(BF16), preferred_element_type=F32)
    kv = lax.dot_general(k_out, vb, (((0,), (0,)), ((), ())), preferred_element_type=F32)
    state_ref[...] = state * jnp.exp(lg * float(C)) + kv
    mu = jnp.mean(y, axis=-1, keepdims=True)
    d = y - mu
    var = jnp.mean(d * d, axis=-1, keepdims=True)
    y_ref[...] = _silu(g_ref[...]) * (d * lax.rsqrt(var + 1e-5))


def _retention(p3, cos, sin, lg_tab, H):
    B, S, _ = p3.shape
    dk = RET_HEAD_DIM
    C = min(512, S)
    nc = S // C
    W = lg_tab.shape[-1]
    blk = lambda off, rev: pl.BlockSpec(
        (None, C, dk),
        (lambda b, h, c: (b, nc - 1 - c, off + h)) if rev else (lambda b, h, c: (b, c, off + h)))
    tab = lambda rev: pl.BlockSpec(
        (C, dk // 2), (lambda b, h, c: (nc - 1 - c, 0)) if rev else (lambda b, h, c: (c, 0)))
    lg_spec = pl.BlockSpec((None, 8, W), lambda b, h, c: (h, 0, 0))
    vmem = 32 << 20
    y_f = pl.pallas_call(
        _ret_fwd_body,
        grid=(B, H, nc),
        in_specs=[blk(0, False), blk(H, False), blk(2 * H, False), tab(False), tab(False), lg_spec],
        out_specs=pl.BlockSpec((None, C, dk), lambda b, h, c: (b, c, h)),
        out_shape=jax.ShapeDtypeStruct((B, S, H * dk), F32),
        scratch_shapes=[pltpu.VMEM((dk, dk), F32), pltpu.VMEM((C, dk), F32), pltpu.VMEM((C, dk), F32),
                        pltpu.VMEM((C, C), F32)],
        compiler_params=_params(("parallel", "parallel", "arbitrary"), vmem),
        name="ret_fwd",
    )(p3, p3, p3, cos, sin, lg_tab)
    return pl.pallas_call(
        _ret_bwd_body,
        grid=(B, H, nc),
        in_specs=[blk(0, True), blk(H, True), blk(2 * H, True), blk(3 * H, True),
                  tab(True), tab(True), lg_spec,
                  pl.BlockSpec((None, C, dk), lambda b, h, c: (b, nc - 1 - c, h))],
        out_specs=pl.BlockSpec((None, C, dk), lambda b, h, c: (b, nc - 1 - c, h)),
        out_shape=jax.ShapeDtypeStruct((B, S, H * dk), F32),
        scratch_shapes=[pltpu.VMEM((dk, dk), F32), pltpu.VMEM((C, dk), F32), pltpu.VMEM((C, dk), F32)],
        compiler_params=_params(("parallel", "parallel", "arbitrary"), vmem),
        name="ret_bwd",
    )(p3, p3, p3, p3, cos, sin, lg_tab, y_f)


HALO = 8


def _fill_haloed(xs_ref, x_ref, S):
    zeros = jnp.zeros((HALO, xs_ref.shape[1]), F32)
    xs_ref[0:HALO, :] = zeros
    xs_ref[S + HALO:S + 2 * HALO, :] = zeros
    xs_ref[HALO:S + HALO, :] = x_ref[...]


def _conv_rows(xs_ref, r0, R, taps, left, bias):
    ext = xs_ref[pl.ds(r0, R + 2 * HALO), :]
    n = R + 2 * HALO
    acc = None
    for j, w in enumerate(taps):
        d = j - left
        sh = ext if d == 0 else pltpu.roll(ext, (-d) % n, axis=0)
        term = w * sh[HALO:HALO + R, :]
        acc = term if acc is None else acc + term
    return acc + bias


FFT_N2 = 128
TN2 = 32
NT = FFT_N2 // TN2
STAGE_CB = LANES


def _store_tiled(o_ref, n1, block):
    for h in range(NT):
        o_ref[h, pl.ds(pl.multiple_of(n1 * TN2, TN2), TN2), :] = block[h * TN2:(h + 1) * TN2, :]


def _hy_conv_body(u_ref, w_ref, b_ref, o_ref, xs_ref, *, S, R, K):
    _fill_haloed(xs_ref, u_ref, S)
    taps = [w_ref[j:j + 1, :] for j in range(K)]
    bias = b_ref[...]
    per = R // FFT_N2

    def body(i, carry):
        r0 = pl.multiple_of(i * R, R)
        res = _conv_rows(xs_ref, r0, R, taps, (K - 1) // 2, bias)
        for q in range(per):
            _store_tiled(o_ref, i * per + q, res[q * FFT_N2:(q + 1) * FFT_N2, :])
        return carry

    lax.fori_loop(0, S // R, body, 0)


def _hy_conv(p3, col0, width, w, b):
    B, S, _ = p3.shape
    K = w.shape[0]
    cb = LANES
    nb = width // cb
    C = width // 3
    R = min(256, S)
    wp = jnp.zeros((8, width), F32).at[:K].set(w)
    off = col0 // cb
    per = C // cb
    return pl.pallas_call(
        functools.partial(_hy_conv_body, S=S, R=R, K=K),
        grid=(B, nb),
        in_specs=[
            pl.BlockSpec((None, S, cb), lambda b_, j: (b_, 0, off + j)),
            pl.BlockSpec((8, cb), lambda b_, j: (0, j)),
            pl.BlockSpec((1, cb), lambda b_, j: (0, j)),
        ],
        out_specs=pl.BlockSpec((None, NT, S // NT, cb), lambda b_, j: ((j // per) * B + b_, 0, 0, j % per)),
        out_shape=jax.ShapeDtypeStruct((3 * B, NT, S // NT, C), F32),
        scratch_shapes=[pltpu.VMEM((S + 2 * HALO, cb), F32)],
        compiler_params=_params(("parallel", "parallel"), 6 * S * cb * 4 + (8 << 20)),
        name="hy_conv",
    )(p3, wp, b.reshape(1, width))


def _dot_split(a, b):
    a_hi = a.astype(BF16).astype(F32)
    b_hi = b.astype(BF16).astype(F32)
    lhs = jnp.concatenate([a_hi, a_hi, a - a_hi], axis=1).astype(BF16)
    rhs = jnp.concatenate([b_hi, b - b_hi, b_hi], axis=0).astype(BF16)
    return jnp.dot(lhs, rhs, preferred_element_type=F32)


def _hy_filter_body(z_ref, w1_ref, b1_ref, f1_ref, w2_ref, b2_ref, f2_ref, w3_ref, dec_ref,
                    o_ref, sum_ref, *, L, tr):
    i = pl.program_id(0)
    z = z_ref[...]
    h = jnp.sin(f1_ref[...] * (_dot_split(z, w1_ref[...]) + b1_ref[...]))
    h = jnp.sin(f2_ref[...] * (_dot_split(h, w2_ref[...]) + b2_ref[...]))
    h = _dot_split(h, w3_ref[...])
    t = z[:, 0:1]
    h = h * jnp.exp(-t * jnp.abs(dec_ref[...]))
    m = i * tr + lax.broadcasted_iota(jnp.int32, h.shape, 0)
    filt = jnp.where(m == L, 0.0, h)
    for q in range(tr // FFT_N2):
        blk = filt[q * FFT_N2:(q + 1) * FFT_N2, :]
        for hh in range(NT):
            o_ref[hh, q * TN2:(q + 1) * TN2, :] = blk[hh * TN2:(hh + 1) * TN2, :]

    @pl.when(i == 0)
    def _():
        sum_ref[...] = jnp.zeros_like(sum_ref)

    sum_ref[...] += jnp.sum(jnp.abs(filt), axis=0, keepdims=True)


def _hy_filter(zfeat, w1, b1, f1, w2, b2, f2, w3, dec, L):
    N, E = zfeat.shape
    hid = w1.shape[1]
    C4 = w3.shape[1]
    tr = min(256, L)
    nt_half = L // tr
    full = lambda a: pl.BlockSpec(a.shape, lambda i: (0,) * a.ndim)
    side = lambda a: pl.BlockSpec((a.shape[0], C4 // 2), lambda i: (0, i // nt_half))
    args = (w1, b1, f1, w2, b2, f2, w3, dec)
    return pl.pallas_call(
        functools.partial(_hy_filter_body, L=L, tr=tr),
        grid=(N // tr,),
        in_specs=[pl.BlockSpec((tr, E), lambda i: (i, 0))] + [full(a) for a in args[:6]]
        + [side(w3), side(dec)],
        out_specs=[pl.BlockSpec((NT, tr // NT, C4 // 2), lambda i: (0, i, 0)),
                   pl.BlockSpec((1, C4 // 2), lambda i: (0, 0))],
        out_shape=[jax.ShapeDtypeStruct((NT, N // NT, C4 // 2), F32),
                   jax.ShapeDtypeStruct((1, C4 // 2), F32)],
        compiler_params=_params(("arbitrary",), 32 << 20),
        name="hy_filter",
    )(zfeat, *args)


def _stage_a_body(w_ref, x_ref, o_ref, *, nparts, rows):
    mo = w_ref.shape[0]
    w = w_ref[...]

    def body(r, carry):
        parts = [x_ref[p, pl.ds(r, rows, stride=TN2), :] for p in range(nparts)]
        x = parts[0] if nparts == 1 else jnp.concatenate(parts, axis=0)
        o_ref[pl.ds(r, mo, stride=TN2), :] = jnp.dot(w, x.astype(BF16), preferred_element_type=F32)
        return carry

    lax.fori_loop(0, TN2, body, 0, unroll=8)


def _stage_a(wmat, x, group, nparts):
    mo, K = wmat.shape
    _, _, T, C = x.shape
    rows = T // TN2
    assert K == nparts * rows
    cb = min(STAGE_CB, C)
    vmem = 2 * nparts * T * cb * 4 + 2 * mo * TN2 * cb * 4 + (12 << 20)
    return pl.pallas_call(
        functools.partial(_stage_a_body, nparts=nparts, rows=rows),
        grid=(C // cb, NT),
        in_specs=[pl.BlockSpec((mo, K), lambda j, h: (0, 0)),
                  pl.BlockSpec((nparts, None, T, cb), lambda j, h: (group, h, 0, j))],
        out_specs=pl.BlockSpec((None, mo * TN2, cb), lambda j, h: (h, 0, j)),
        out_shape=jax.ShapeDtypeStruct((NT, mo * TN2, C), F32),
        compiler_params=_params(("parallel", "parallel"), vmem),
        name="hy_stage_a",
    )(wmat, x)


def _hy_freq_body(g_ref, gi_ref, x_ref, f_ref, s_ref, o_ref, *, kb):
    n2 = FFT_N2
    inv = 1.0 / (s_ref[...] + 1e-6)

    def gather(ref, t):
        return jnp.concatenate([ref[h, ri, t] for ri in range(2) for h in range(NT)], axis=0)

    for t in range(kb):
        g = g_ref[t]
        y = jnp.dot(g, gather(x_ref, t).astype(BF16), preferred_element_type=F32)
        h = jnp.dot(g, gather(f_ref, t).astype(BF16), preferred_element_type=F32) * inv
        yr, yi = y[:n2], y[n2:]
        hr, hi = h[:n2], h[n2:]
        z = jnp.concatenate([yr * hr - yi * hi, yr * hi + yi * hr], axis=0).astype(BF16)
        e = jnp.dot(gi_ref[t], z, preferred_element_type=F32)
        for ri in range(2):
            for hh in range(NT):
                o_ref[hh, ri, t] = e[ri * n2 + hh * TN2:ri * n2 + (hh + 1) * TN2]


def _hy_freq(gmat, gimat, xa, fa, asum, order, C):
    N1 = xa.shape[2]
    n2 = FFT_N2
    kb = min(4, N1)
    blk = pl.BlockSpec((NT, 2, kb, TN2, C), lambda i: (0, 0, i, 0, 0))
    gspec = pl.BlockSpec((kb, 2 * n2, 2 * n2), lambda i: (i, 0, 0))
    return pl.pallas_call(
        functools.partial(_hy_freq_body, kb=kb),
        grid=(N1 // kb,),
        in_specs=[gspec, gspec, blk,
                  pl.BlockSpec((NT, 2, kb, TN2, C), lambda i: (0, 0, i, 0, order)),
                  pl.BlockSpec((1, C), lambda i: (0, order))],
        out_specs=blk,
        out_shape=jax.ShapeDtypeStruct(xa.shape, F32),
        compiler_params=_params(("parallel",), 6 * 2 * kb * n2 * C * 4 + 24 * n2 * C * 4 + (8 << 20)),
        name="hy_freq",
    )(gmat, gimat, xa, fa, asum)


def _stage_d_body(w_ref, e_ref, z_ref, x_ref, b_ref, o_ref, *, B, nh):
    w = w_ref[...]
    bias = b_ref[...]
    mi = w.shape[1]

    def body(r, carry):
        e = e_ref[pl.ds(r, mi, stride=TN2), :].astype(BF16)
        conv = jnp.dot(w, e, preferred_element_type=F32)
        rows = pl.ds(r, nh, stride=TN2)
        for b in range(B):
            o_ref[b, rows, :] = x_ref[b, rows, :] * (conv[b * nh:(b + 1) * nh] + bias * z_ref[b, rows, :])
        return carry

    lax.fori_loop(0, TN2, body, 0, unroll=8)


def _stage_d(w4, e, z, z_group, gate, gate_group, bias, B):
    mo, mi = w4.shape
    nh = mo // B
    T, C = z.shape[2], z.shape[3]
    cb = min(STAGE_CB, C)
    nat = lambda grp: pl.BlockSpec((B, None, T, cb), lambda j, h: (grp, h, 0, j))
    vmem = 2 * mi * TN2 * cb * 4 + 6 * B * T * cb * 4 + (12 << 20)
    return pl.pallas_call(
        functools.partial(_stage_d_body, B=B, nh=nh),
        grid=(C // cb, NT),
        in_specs=[pl.BlockSpec((mo, mi), lambda j, h: (0, 0)),
                  pl.BlockSpec((None, mi * TN2, cb), lambda j, h: (h, 0, j)),
                  nat(z_group), nat(gate_group),
                  pl.BlockSpec((1, cb), lambda j, h: (0, j))],
        out_specs=nat(0),
        out_shape=jax.ShapeDtypeStruct((B, NT, T, C), F32),
        compiler_params=_params(("parallel", "parallel"), vmem),
        name="hy_stage_d",
    )(w4, e, z, gate, bias)


def _dft_tables(S):
    N = 2 * S
    n2 = FFT_N2
    N1 = N // n2
    nh = S // n2
    k1 = np.arange(N1)
    f1 = np.exp(-2j * np.pi * np.outer(k1, np.arange(N1)) / N1)
    fa = f1[:, :nh]
    wa = np.block([[fa.real, -fa.imag], [fa.imag, fa.real]])
    wa_real = np.concatenate([f1.real, f1.imag], axis=0)
    a2 = np.arange(n2)
    base = np.exp(-2j * np.pi * np.outer(a2, a2) / n2)
    tw = np.exp(-2j * np.pi * np.outer(k1, a2) / N)
    g = base[None, :, :] * tw[:, None, :]
    gi = np.conj(np.transpose(g, (0, 2, 1)))
    blockify = lambda c: np.concatenate(
        [np.concatenate([c.real, -c.imag], axis=-1), np.concatenate([c.imag, c.real], axis=-1)], axis=-2)
    f4 = np.exp(2j * np.pi * np.outer(np.arange(nh), k1) / N1) / N
    wd = np.block([[f4.real, -f4.imag], [f4.imag, f4.real]])
    cast = lambda a: jnp.asarray(a, dtype=F32).astype(BF16)
    return dict(N1=N1, nh=nh, wa=cast(wa), wa_real=cast(wa_real), g=cast(blockify(g)),
                gi=cast(blockify(gi)), wd=cast(wd))


def _hyena(p3, col0, C, tabs, conv_w, conv_b, filt_params, hy_bias, zfeat):
    B, S, _ = p3.shape
    N1 = tabs["N1"]
    u = _hy_conv(p3, col0, 3 * C, conv_w, conv_b)
    filt, asum = _hy_filter(zfeat, *filt_params, S)
    fa = _stage_a(tabs["wa_real"], filt[None], 0, 1).reshape(NT, 2, N1, TN2, 2 * C)
    z = u
    for order in range(2):
        xa = _stage_a(tabs["wa"], z, 0, B)
        e = _hy_freq(tabs["g"], tabs["gi"], xa.reshape(NT, 2, N1, TN2, C), fa, asum, order, C)
        z = _stage_d(tabs["wd"], e.reshape(NT, 2 * N1 * TN2, C), z, 0, u, 1 + order,
                     hy_bias[order].reshape(1, C), B)
    return z


def _scan_rows(a, b, reverse):
    R = a.shape[0]
    row = lax.broadcasted_iota(jnp.int32, a.shape, 0)
    d = 1
    while d < R:
        if reverse:
            a_s = pltpu.roll(a, R - d, axis=0)
            b_s = pltpu.roll(b, R - d, axis=0)
            m = row < R - d
        else:
            a_s = pltpu.roll(a, d, axis=0)
            b_s = pltpu.roll(b, d, axis=0)
            m = row >= d
        b = jnp.where(m, a * b_s + b, b)
        a = jnp.where(m, a * a_s, a)
        d *= 2
    return a, b


def _gelu_tanh(x):
    return 0.5 * x * (1.0 + jnp.tanh(0.7978845608028654 * (x + 0.044715 * x * x * x)))


def _lru_body(g_ref, x_ref, cw_ref, cb_ref, w_ref, bias_ref, lam_ref, o_ref,
              xs_ref, hf_ref, ab_ref, bb_ref, *, S, R, K):
    bd = x_ref.shape[1]
    _fill_haloed(xs_ref, x_ref, S)
    taps = [cw_ref[j:j + 1, :] for j in range(K)]
    cbias = cb_ref[...]
    w = w_ref[...]
    bias = bias_ref[...]
    nl = -lam_ref[...]
    sp = jnp.maximum(nl, 0.0) + jnp.log1p(jnp.exp(-jnp.abs(nl)))
    sp0 = sp[0:1, :]
    sp1 = sp[1:2, :]
    nchunk = S // R

    def gates(r, i, spd, xr):
        log_a = -LRU_C * jax.nn.sigmoid(r) * spd
        a = jnp.exp(log_a)
        b = jnp.sqrt(1.0 - a * a) * (jax.nn.sigmoid(i) * xr)
        return a, b

    def fwd(i, hc):
        r0 = pl.multiple_of(i * R, R)
        xr = _conv_rows(xs_ref, r0, R, taps, (K - 1) // 2, cbias)
        proj = jnp.dot(xr.astype(BF16), w, preferred_element_type=F32) + bias
        a0, b0 = gates(proj[:, 0:bd], proj[:, bd:2 * bd], sp0, xr)
        a1, b1 = gates(proj[:, 2 * bd:3 * bd], proj[:, 3 * bd:4 * bd], sp1, xr)
        ab_ref[pl.ds(r0, R), :] = a1
        bb_ref[pl.ds(r0, R), :] = b1
        ac, hl = _scan_rows(a0, b0, reverse=False)
        h = ac * hc + hl
        hf_ref[pl.ds(r0, R), :] = h
        return h[R - 1:R, :]

    lax.fori_loop(0, nchunk, fwd, jnp.zeros((1, bd), F32))

    def bwd(t, hc):
        r0 = pl.multiple_of((nchunk - 1 - t) * R, R)
        sl = pl.ds(r0, R)
        ac, hl = _scan_rows(ab_ref[sl, :], bb_ref[sl, :], reverse=True)
        h = ac * hc + hl
        o_ref[sl, :] = _gelu_tanh(g_ref[sl, :]) * (hf_ref[sl, :] + h)
        return h[0:1, :]

    lax.fori_loop(0, nchunk, bwd, jnp.zeros((1, bd), F32))


def _lru(p3, col_g, col_x, width, conv_w, conv_b, wa, ba, wx, bx, lam):
    B, S, _ = p3.shape
    bd = LRU_BLOCK_DIM
    nb = width // bd
    K = conv_w.shape[0]
    R = min(256, S)
    cw = jnp.zeros((8, width), F32).at[:K].set(conv_w)
    wcat = jnp.concatenate([wa[0], wx[0], wa[1], wx[1]], axis=-1).astype(BF16)
    bsplit = lambda v: v.reshape(nb, 1, bd)
    bcat = jnp.concatenate([bsplit(ba[0]), bsplit(bx[0]), bsplit(ba[1]), bsplit(bx[1])], axis=-1)
    lam8 = jnp.zeros((8, width), F32).at[:2].set(lam)
    og, ox = col_g // bd, col_x // bd
    return pl.pallas_call(
        functools.partial(_lru_body, S=S, R=R, K=K),
        grid=(B, nb),
        in_specs=[
            pl.BlockSpec((None, S, bd), lambda b, j: (b, 0, og + j)),
            pl.BlockSpec((None, S, bd), lambda b, j: (b, 0, ox + j)),
            pl.BlockSpec((8, bd), lambda b, j: (0, j)),
            pl.BlockSpec((1, bd), lambda b, j: (0, j)),
            pl.BlockSpec((None, bd, 4 * bd), lambda b, j: (j, 0, 0)),
            pl.BlockSpec((None, 1, 4 * bd), lambda b, j: (j, 0, 0)),
            pl.BlockSpec((8, bd), lambda b, j: (0, j)),
        ],
        out_specs=pl.BlockSpec((None, S, bd), lambda b, j: (b, 0, j)),
        out_shape=jax.ShapeDtypeStruct((B, S, width), F32),
        scratch_shapes=[pltpu.VMEM((S + 2 * HALO, bd), F32), pltpu.VMEM((S, bd), F32),
                        pltpu.VMEM((S, bd), F32), pltpu.VMEM((S, bd), F32)],
        compiler_params=_params(("parallel", "parallel"), 10 * S * bd * 4 + (12 << 20)),
        name="lru",
    )(p3, p3, cw, conv_b.reshape(1, width), wcat, bcat, lam8)


def _mixnorm_body(a_ref, b_ref, c_ref, g_ref, o_ref):
    off = 0
    for ref in (a_ref, b_ref, c_ref):
        if len(ref.shape) == 3:
            y = jnp.concatenate([ref[h] for h in range(NT)], axis=0)
        else:
            y = ref[...]
        w = y.shape[1]
        ms = jnp.mean(y * y, axis=-1, keepdims=True)
        o_ref[:, off:off + w] = (y * lax.rsqrt(ms + 1e-6) * g_ref[:, off:off + w]).astype(BF16)
        off += w


def _mixnorm(ya, yb_tiled, yc, gain):
    M = ya.shape[0]
    B, _, T, wb = yb_tiled.shape
    nh = T // TN2
    widths = (ya.shape[1], wb, yc.shape[1])
    D = sum(widths)
    tm = FFT_N2
    nat = lambda w: pl.BlockSpec((tm, w), lambda i: (i, 0))
    tiled = pl.BlockSpec((None, NT, None, TN2, wb), lambda i: (i // nh, 0, i % nh, 0, 0))
    return pl.pallas_call(
        _mixnorm_body,
        grid=(M // tm,),
        in_specs=[nat(widths[0]), tiled, nat(widths[2]), pl.BlockSpec((1, D), lambda i: (0, 0))],
        out_specs=pl.BlockSpec((tm, D), lambda i: (i, 0)),
        out_shape=jax.ShapeDtypeStruct((M, D), BF16),
        compiler_params=_params(("parallel",), 32 << 20),
        name="mixnorm",
    )(ya, yb_tiled.reshape(B, NT, nh, TN2, wb), yc, gain)


def kernel(x, ffa_w_gate, ffa_w_up, ffa_w_down, ffb_w_gate, ffb_w_up, ffb_w_down, ln_gain, ln_bias,
           w_in, w_out, mix_norm_gain, hy_conv_w, hy_conv_b, hy_filt_w1, hy_filt_b1, hy_filt_freq1,
           hy_filt_w2, hy_filt_b2, hy_filt_freq2, hy_filt_w3, hy_decay, hy_bias, lru_conv_w,
           lru_conv_b, lru_wa, lru_ba, lru_wx, lru_bx, lru_lambda):
    B, S, D = x.shape
    depth = w_in.shape[0]
    M = B * S
    alpha = (2 * depth) ** 0.25
    R = D // 2
    H = R // RET_HEAD_DIM
    Hy = D // 4
    Lw = D // 4
    col_hy = 4 * R
    col_lg = col_hy + 3 * Hy
    col_lx = col_lg + Lw

    pos = jnp.arange(S, dtype=F32)
    half = RET_HEAD_DIM // 2
    inv_freq = 1.0 / (10000.0 ** jnp.linspace(0.0, 1.0, half, dtype=F32))
    ang = pos[:, None] * inv_freq[None, :]
    cos, sin = jnp.cos(ang), jnp.sin(ang)
    log_g = jnp.log1p(-jnp.exp2(-5.0 - jnp.arange(H, dtype=F32)))
    lg_tab = jnp.broadcast_to(log_g[:, None, None], (H, 8, max(512, RET_HEAD_DIM)))
    tabs = _dft_tables(S)
    m = jnp.arange(2 * S)
    fpos = jnp.where(m < S, m, 2 * S - m).astype(F32)
    emb = hy_filt_w1.shape[1]
    bands = (emb - 1) // 2
    fr = jnp.linspace(1e-4, bands - 1, bands, dtype=F32)
    wang = 2.0 * math.pi * fpos / S
    zfeat = jnp.concatenate([(fpos / max(S - 1, 1))[:, None], jnp.cos(wang[:, None] * fr),
                             -jnp.sin(wang[:, None] * fr)], axis=-1)
    epad = LANES
    zfeat = jnp.pad(zfeat, ((0, 0), (0, epad - emb)))

    assert B == 2, "the Hyena DFT carries the two batch rows as one complex signal"
    ffa = (_pack_gate_up(ffa_w_gate, ffa_w_up), _to_bf16(ffa_w_down))
    ffb = (_pack_gate_up(ffb_w_gate, ffb_w_up), _to_bf16(ffb_w_down))
    w_in_b, w_out_b = _to_bf16(w_in), _to_bf16(w_out)
    row = lambda v: v.reshape(1, -1)

    h = x.reshape(M, D)
    for l in range(depth):
        h = _ffn(h, *ffa, l, row(ln_gain[l, 0]), row(ln_bias[l, 0]), alpha)
        p3 = _inproj(h, w_in_b, l).reshape(B, S, -1)
        y_ret = _retention(p3, cos, sin, lg_tab, H)
        w1p = jnp.pad(hy_filt_w1[l], ((0, epad - emb), (0, 0)))
        filt_params = (w1p, row(hy_filt_b1[l]), row(hy_filt_freq1[l]), hy_filt_w2[l],
                       row(hy_filt_b2[l]), row(hy_filt_freq2[l]), hy_filt_w3[l], row(hy_decay[l]))
        y_hy = _hyena(p3, col_hy, Hy, tabs, hy_conv_w[l], hy_conv_b[l], filt_params, hy_bias[l], zfeat)
        y_lru = _lru(p3, col_lg, col_lx, Lw, lru_conv_w[l], lru_conv_b[l], lru_wa[l], lru_ba[l],
                     lru_wx[l], lru_bx[l], lru_lambda[l])
        yb = _mixnorm(y_ret.reshape(M, R), y_hy, y_lru.reshape(M, Lw),
                      row(mix_norm_gain[l]))
        h = _outproj(yb, w_out_b, l, h, row(ln_gain[l, 1]), row(ln_bias[l, 1]), alpha)
        h = _ffn(h, *ffb, l, row(ln_gain[l, 2]), row(ln_bias[l, 2]), alpha)
    return h.reshape(B, S, D)
```

```python
import functools
import math

import numpy as np
import jax
import jax.numpy as jnp
from jax import lax
from jax.experimental import pallas as pl
from jax.experimental.pallas import tpu as pltpu

F32 = jnp.float32
BF16 = jnp.bfloat16

LANES = 128
RET_HEAD_DIM = 256
LRU_BLOCK_DIM = 128
LRU_C = 8.0
VMEM_CAP = 60 * 1024 * 1024


def _params(semantics, vmem_bytes):
    return pltpu.CompilerParams(
        dimension_semantics=semantics,
        vmem_limit_bytes=int(min(VMEM_CAP, max(vmem_bytes, 16 * 1024 * 1024))))


def _layer_norm_rows(y, g, b, eps=1e-5):
    mu = jnp.mean(y, axis=-1, keepdims=True)
    d = y - mu
    var = jnp.mean(d * d, axis=-1, keepdims=True)
    return d * lax.rsqrt(var + eps) * g + b


def _silu(x):
    return x * jax.nn.sigmoid(x)


def _ffn_body(x_ref, wgu_a, wgu_b, wd_a, wd_b, g_ref, b_ref, o_ref, xb_ref, *, alpha, rows, slab, nf):
    j = pl.program_id(1)
    nj = pl.num_programs(1)
    tm = x_ref.shape[0]
    tf = wd_a.shape[0]
    D = o_ref.shape[1]

    @pl.when(j == 0)
    def _():
        xb_ref[...] = x_ref[...].astype(BF16)
        o_ref[...] = jnp.zeros_like(o_ref)

    def hidden(wgu_ref):
        gu = jnp.dot(xb_ref[...], wgu_ref[...], preferred_element_type=F32)
        return (_silu(gu[:, :tf]) * gu[:, tf:]).astype(BF16)

    def accumulate(pairs):
        hs = [(hidden(wgu_ref), wd_ref) for wgu_ref, wd_ref in pairs]
        for c0 in range(0, D, slab):
            part = None
            for h, wd_ref in hs:
                p = jnp.dot(h, wd_ref[:, c0:c0 + slab], preferred_element_type=F32)
                part = p if part is None else part + p
            o_ref[:, c0:c0 + slab] += part

    if nf % 2 == 0:
        accumulate([(wgu_a, wd_a), (wgu_b, wd_b)])
    else:
        @pl.when(2 * j + 1 < nf)
        def _():
            accumulate([(wgu_a, wd_a), (wgu_b, wd_b)])

        @pl.when(2 * j + 1 >= nf)
        def _():
            accumulate([(wgu_a, wd_a)])

    @pl.when(j == nj - 1)
    def _():
        g = g_ref[...]
        b = b_ref[...]

        def body(r, carry):
            sl = pl.ds(pl.multiple_of(r * rows, rows), rows)
            y = alpha * x_ref[sl, :] + 0.5 * o_ref[sl, :]
            o_ref[sl, :] = _layer_norm_rows(y, g, b)
            return carry

        lax.fori_loop(0, tm // rows, body, 0, unroll=2)


def _cast_body(x_ref, o_ref):
    o_ref[...] = x_ref[...].astype(BF16)


def _to_bf16(w):
    L, R, C = w.shape
    tr = 256
    while tr > 8 and (R % tr or tr * C * 4 > (12 << 20)):
        tr //= 2
    spec = pl.BlockSpec((None, tr, C), lambda l, i: (l, i, 0))
    return pl.pallas_call(
        _cast_body,
        grid=(L, R // tr),
        in_specs=[spec],
        out_specs=spec,
        out_shape=jax.ShapeDtypeStruct(w.shape, BF16),
        compiler_params=_params(("parallel", "parallel"), 3 * tr * C * 4 + (8 << 20)),
        name="to_bf16",
    )(w)


FFN_TF = 256


def _pack_gate_up_body(g_ref, u_ref, o_ref, *, tf):
    for c in range(g_ref.shape[1] // tf):
        o_ref[c, :, :tf] = g_ref[:, c * tf:(c + 1) * tf].astype(BF16)
        o_ref[c, :, tf:] = u_ref[:, c * tf:(c + 1) * tf].astype(BF16)


def _pack_gate_up(wg, wu):
    L, D, F = wg.shape
    tf = FFN_TF if F % FFN_TF == 0 else F
    tr = min(128, D)
    spec = pl.BlockSpec((None, tr, F), lambda l, i: (l, i, 0))
    return pl.pallas_call(
        functools.partial(_pack_gate_up_body, tf=tf),
        grid=(L, D // tr),
        in_specs=[spec, spec],
        out_specs=pl.BlockSpec((None, F // tf, tr, 2 * tf), lambda l, i: (l, 0, i, 0)),
        out_shape=jax.ShapeDtypeStruct((L, F // tf, D, 2 * tf), BF16),
        compiler_params=_params(("parallel", "parallel"), 6 * tr * F * 4 + (8 << 20)),
        name="pack_gate_up",
    )(wg, wu)


def _ffn(x, wgu, wd, layer, g, b, alpha):
    M, D = x.shape
    F = wd.shape[1]
    tm = min(512, M)
    tf = FFN_TF if F % FFN_TF == 0 else F
    nf = F // tf
    rows = min(16, tm)
    slab = 1024 if D % 1024 == 0 else D
    vmem = (tm * D * 4 + tm * D * 2 + 2 * tm * D * 4 + 2 * 2 * 3 * D * tf * 2 + 2 * 6 * tm * tf * 4
            + 2 * tm * slab * 4)
    blk_a = lambda i, j: 2 * j
    blk_b = lambda i, j: jnp.minimum(2 * j + 1, nf - 1)
    wgu_spec = lambda blk: pl.BlockSpec((None, None, D, 2 * tf), lambda i, j: (layer, blk(i, j), 0, 0))
    wd_spec = lambda blk: pl.BlockSpec((None, tf, D), lambda i, j: (layer, blk(i, j), 0))
    return pl.pallas_call(
        functools.partial(_ffn_body, alpha=alpha, rows=rows, slab=slab, nf=nf),
        grid=(M // tm, (nf + 1) // 2),
        in_specs=[
            pl.BlockSpec((tm, D), lambda i, j: (i, 0), pipeline_mode=pl.Buffered(1)),
            wgu_spec(blk_a), wgu_spec(blk_b), wd_spec(blk_a), wd_spec(blk_b),
            pl.BlockSpec((1, D), lambda i, j: (0, 0)),
            pl.BlockSpec((1, D), lambda i, j: (0, 0)),
        ],
        out_specs=pl.BlockSpec((tm, D), lambda i, j: (i, 0)),
        out_shape=jax.ShapeDtypeStruct((M, D), F32),
        scratch_shapes=[pltpu.VMEM((tm, D), BF16)],
        compiler_params=_params(("parallel", "arbitrary"), vmem + (2 << 20)),
        name="ffn",
    )(x, wgu, wgu, wd, wd, g, b)


def _inproj_body(x_ref, w_ref, o_ref, xb_ref):
    @pl.when(pl.program_id(1) == 0)
    def _():
        xb_ref[...] = x_ref[...].astype(BF16)

    o_ref[...] = jnp.dot(xb_ref[...], w_ref[...], preferred_element_type=F32)


def _inproj(x, w, layer):
    M, K = x.shape
    N = w.shape[2]
    tm = min(512, M)
    tn = 1024 if N % 1024 == 0 else (512 if N % 512 == 0 else 128)
    vmem = 2 * tm * K * 4 + tm * K * 2 + 2 * K * tn * 2 + 2 * tm * tn * 4
    return pl.pallas_call(
        _inproj_body,
        grid=(M // tm, N // tn),
        in_specs=[
            pl.BlockSpec((tm, K), lambda i, j: (i, 0)),
            pl.BlockSpec((None, K, tn), lambda i, j: (layer, 0, j)),
        ],
        out_specs=pl.BlockSpec((tm, tn), lambda i, j: (i, j)),
        out_shape=jax.ShapeDtypeStruct((M, N), F32),
        scratch_shapes=[pltpu.VMEM((tm, K), BF16)],
        compiler_params=_params(("parallel", "arbitrary"), vmem + (4 << 20)),
        name="inproj",
    )(x, w)


def _outproj_body(y_ref, w_ref, x_ref, g_ref, b_ref, o_ref, *, alpha, rows, nj, slab):
    j = pl.program_id(1)
    tm, D = o_ref.shape
    rc = x_ref.shape[0]

    @pl.when(j == 0)
    def _():
        o_ref[...] = jnp.zeros_like(o_ref)

    yk = y_ref[...]
    for c0 in range(0, D, slab):
        o_ref[:, c0:c0 + slab] += jnp.dot(yk, w_ref[:, c0:c0 + slab], preferred_element_type=F32)
    sl = pl.ds(pl.multiple_of(j * rc, rc), rc)
    o_ref[sl, :] += alpha * x_ref[...]

    @pl.when(j == nj - 1)
    def _():
        g = g_ref[...]
        b = b_ref[...]

        def body(r, carry):
            rs = pl.ds(pl.multiple_of(r * rows, rows), rows)
            o_ref[rs, :] = _layer_norm_rows(o_ref[rs, :], g, b)
            return carry

        lax.fori_loop(0, tm // rows, body, 0, unroll=2)


def _outproj(y, w, layer, x, g, b, alpha):
    M, K = y.shape
    D = w.shape[2]
    tm = min(512, M)
    tk = 1024 if K % 1024 == 0 else K
    nj = K // tk
    rc = tm // nj
    rows = min(16, tm)
    slab = 1024 if D % 1024 == 0 else D
    vmem = 2 * tm * tk * 2 + 2 * tk * D * 2 + 2 * rc * D * 4 + 2 * tm * D * 4 + 2 * tm * slab * 4
    return pl.pallas_call(
        functools.partial(_outproj_body, alpha=alpha, rows=rows, nj=nj, slab=slab),
        grid=(M // tm, nj),
        in_specs=[
            pl.BlockSpec((tm, tk), lambda i, j: (i, j)),
            pl.BlockSpec((None, tk, D), lambda i, j: (layer, j, 0)),
            pl.BlockSpec((rc, D), lambda i, j: (i * nj + j, 0)),
            pl.BlockSpec((1, D), lambda i, j: (0, 0)),
            pl.BlockSpec((1, D), lambda i, j: (0, 0)),
        ],
        out_specs=pl.BlockSpec((tm, D), lambda i, j: (i, 0)),
        out_shape=jax.ShapeDtypeStruct((M, D), F32),
        compiler_params=_params(("parallel", "arbitrary"), vmem + (6 << 20)),
        name="outproj",
    )(y, w, x, g, b)


def _rotary(x, cos, sin):
    half = x.shape[-1] // 2
    x1 = x[:, :half]
    x2 = x[:, half:]
    return jnp.concatenate([x1 * cos - x2 * sin, x1 * sin + x2 * cos], axis=-1)


def _ret_qkv(q_ref, k_ref, cos_ref, sin_ref, chunk):
    C = q_ref.shape[0]
    rows = pl.ds(pl.multiple_of(chunk * C, C), C)
    cos = cos_ref[rows, :]
    sin = sin_ref[rows, :]
    q = _rotary(q_ref[...], cos, sin)
    k = _rotary(k_ref[...], cos, sin) * (RET_HEAD_DIM ** -0.5)
    return q, k


def _row_index(shape):
    return lax.broadcasted_iota(jnp.int32, shape, 0).astype(F32)


def _ret_fwd_body(q_ref, k_ref, v_ref, cos_ref, sin_ref, lg_ref, y_ref,
                  state_ref, qd_ref, kd_ref, intra_ref):
    C, dk = q_ref.shape
    lg = lg_ref[0:1, :dk]

    @pl.when(pl.program_id(2) == 0)
    def _():
        state_ref[...] = jnp.zeros_like(state_ref)
        row = _row_index((C, dk))
        qd_ref[...] = jnp.exp(lg * (row + 1.0))
        kd_ref[...] = jnp.exp(lg * (C - 1.0 - row))
        ii = lax.broadcasted_iota(jnp.int32, (C, C), 0)
        jj = lax.broadcasted_iota(jnp.int32, (C, C), 1)
        intra_ref[...] = jnp.exp(lg_ref[0:1, :C] * jnp.abs(ii - jj).astype(F32))

    q, k = _ret_qkv(q_ref, k_ref, cos_ref, sin_ref, pl.program_id(2))
    vb = v_ref[...].astype(BF16)
    q_in = (q * qd_ref[...]).astype(BF16)
    k_out = (k * kd_ref[...]).astype(BF16)
    s = lax.dot_general(q.astype(BF16), k.astype(BF16), (((1,), (1,)), ((), ())),
                        preferred_element_type=F32) * intra_ref[...]
    state = state_ref[...]
    y = jnp.dot(s.astype(BF16), vb, preferred_element_type=F32)
    y = y + jnp.dot(q_in, state.astype(BF16), preferred_element_type=F32)
    y_ref[...] = y
    kv = lax.dot_general(k_out, vb, (((0,), (0,)), ((), ())), preferred_element_type=F32)
    state_ref[...] = state * jnp.exp(lg * float(C)) + kv


def _ret_bwd_body(q_ref, k_ref, v_ref, g_ref, cos_ref, sin_ref, lg_ref, yf_ref, y_ref,
                  state_ref, qd_ref, kd_ref):
    C, dk = q_ref.shape
    lg = lg_ref[0:1, :dk]

    @pl.when(pl.program_id(2) == 0)
    def _():
        state_ref[...] = jnp.zeros_like(state_ref)
        row = _row_index((C, dk))
        qd_ref[...] = jnp.exp(lg * (C - row))
        kd_ref[...] = jnp.exp(lg * row)

    q, k = _ret_qkv(q_ref, k_ref, cos_ref, sin_ref, pl.num_programs(2) - 1 - pl.program_id(2))
    vb = v_ref[...].astype(BF16)
    q_in = (q * qd_ref[...]).astype(BF16)
    k_out = (k * kd_ref[...]).astype(BF16)
    state = state_ref[...]
    y = yf_ref[...] + jnp.dot(q_in, state.astype(BF16), preferred_element_type=F32)
    kv = lax.dot_general(k_out, vb, (((0,), (0,)), ((), ())), preferred_element_type=F32)
    state_ref[...] = state * jnp.exp(lg * float(C)) + kv
    mu = jnp.mean(y, axis=-1, keepdims=True)
    d = y - mu
    var = jnp.mean(d * d, axis=-1, keepdims=True)
    y_ref[...] = _silu(g_ref[...]) * (d * lax.rsqrt(var + 1e-5))


def _retention(p3, cos, sin, lg_tab, H):
    B, S, _ = p3.shape
    dk = RET_HEAD_DIM
    C = min(512, S)
    nc = S // C
    W = lg_tab.shape[-1]
    blk = lambda off, rev: pl.BlockSpec(
        (None, C, dk),
        (lambda b, h, c: (b, nc - 1 - c, off + h)) if rev else (lambda b, h, c: (b, c, off + h)))
    tab = lambda rev: pl.BlockSpec((S, dk // 2), lambda b, h, c: (0, 0))
    lg_spec = pl.BlockSpec((None, 8, W), lambda b, h, c: (h, 0, 0))
    vmem = 32 << 20
    y_f = pl.pallas_call(
        _ret_fwd_body,
        grid=(B, H, nc),
        in_specs=[blk(0, False), blk(H, False), blk(2 * H, False), tab(False), tab(False), lg_spec],
        out_specs=pl.BlockSpec((None, C, dk), lambda b, h, c: (b, c, h)),
        out_shape=jax.ShapeDtypeStruct((B, S, H * dk), F32),
        scratch_shapes=[pltpu.VMEM((dk, dk), F32), pltpu.VMEM((C, dk), F32), pltpu.VMEM((C, dk), F32),
                        pltpu.VMEM((C, C), F32)],
        compiler_params=_params(("parallel", "parallel", "arbitrary"), vmem),
        name="ret_fwd",
    )(p3, p3, p3, cos, sin, lg_tab)
    return pl.pallas_call(
        _ret_bwd_body,
        grid=(B, H, nc),
        in_specs=[blk(0, True), blk(H, True), blk(2 * H, True), blk(3 * H, True),
                  tab(True), tab(True), lg_spec,
                  pl.BlockSpec((None, C, dk), lambda b, h, c: (b, nc - 1 - c, h))],
        out_specs=pl.BlockSpec((None, C, dk), lambda b, h, c: (b, nc - 1 - c, h)),
        out_shape=jax.ShapeDtypeStruct((B, S, H * dk), F32),
        scratch_shapes=[pltpu.VMEM((dk, dk), F32), pltpu.VMEM((C, dk), F32), pltpu.VMEM((C, dk), F32)],
        compiler_params=_params(("parallel", "parallel", "arbitrary"), vmem),
        name="ret_bwd",
    )(p3, p3, p3, p3, cos, sin, lg_tab, y_f)


HALO = 8


def _fill_haloed(xs_ref, x_ref, S):
    zeros = jnp.zeros((HALO, xs_ref.shape[1]), F32)
    xs_ref[0:HALO, :] = zeros
    xs_ref[S + HALO:S + 2 * HALO, :] = zeros
    xs_ref[HALO:S + HALO, :] = x_ref[...]


def _conv_rows(xs_ref, r0, R, taps, left, bias):
    ext = xs_ref[pl.ds(r0, R + 2 * HALO), :]
    n = R + 2 * HALO
    acc = None
    for j, w in enumerate(taps):
        d = j - left
        sh = ext if d == 0 else pltpu.roll(ext, (-d) % n, axis=0)
        term = w * sh[HALO:HALO + R, :]
        acc = term if acc is None else acc + term
    return acc + bias


FFT_N2 = 128
TN2 = 32
NT = FFT_N2 // TN2
PITCH = TN2 + 4
STAGE_CB = LANES


def _store_tiled_pair(o_ref, lead, blocks):
    pad = jnp.zeros((PITCH - TN2, blocks[0].shape[1]), F32)
    for h in range(NT):
        for q, block in enumerate(blocks):
            o_ref[(h, *lead, slice(q * PITCH, q * PITCH + TN2))] = block[h * TN2:(h + 1) * TN2, :]
            o_ref[(h, *lead, slice(q * PITCH + TN2, (q + 1) * PITCH))] = pad


def _hy_conv_body(u_ref, w_ref, b_ref, o_ref, xs_ref, *, S, R, K):
    _fill_haloed(xs_ref, u_ref, S)
    taps = [w_ref[j:j + 1, :] for j in range(K)]
    bias = b_ref[...]

    def body(i, carry):
        r0 = pl.multiple_of(i * R, R)
        res = _conv_rows(xs_ref, r0, R, taps, (K - 1) // 2, bias)
        _store_tiled_pair(o_ref, (i,), [res[:FFT_N2], res[FFT_N2:]])
        return carry

    lax.fori_loop(0, S // R, body, 0)


def _hy_conv(p3, col0, width, w, b):
    B, S, _ = p3.shape
    K = w.shape[0]
    cb = LANES
    nb = width // cb
    C = width // 3
    R = 2 * FFT_N2
    npair = S // R
    wp = jnp.zeros((8, width), F32).at[:K].set(w)
    off = col0 // cb
    per = C // cb
    return pl.pallas_call(
        functools.partial(_hy_conv_body, S=S, R=R, K=K),
        grid=(B, nb),
        in_specs=[
            pl.BlockSpec((None, S, cb), lambda b_, j: (b_, 0, off + j)),
            pl.BlockSpec((8, cb), lambda b_, j: (0, j)),
            pl.BlockSpec((1, cb), lambda b_, j: (0, j)),
        ],
        out_specs=pl.BlockSpec((None, NT, npair, 2 * PITCH, cb),
                               lambda b_, j: ((j // per) * B + b_, 0, 0, 0, j % per)),
        out_shape=jax.ShapeDtypeStruct((3 * B, NT, npair, 2 * PITCH, C), F32),
        scratch_shapes=[pltpu.VMEM((S + 2 * HALO, cb), F32)],
        compiler_params=_params(("parallel", "parallel"), 7 * S * cb * 4 + (8 << 20)),
        name="hy_conv",
    )(p3, wp, b.reshape(1, width)).reshape(3 * B, NT, npair * 2 * PITCH, C)


def _dot_split(a, b):
    a_hi = a.astype(BF16).astype(F32)
    b_hi = b.astype(BF16).astype(F32)
    lhs = jnp.concatenate([a_hi, a_hi, a - a_hi], axis=1).astype(BF16)
    rhs = jnp.concatenate([b_hi, b - b_hi, b_hi], axis=0).astype(BF16)
    return jnp.dot(lhs, rhs, preferred_element_type=F32)


def _hy_filter_body(z_ref, w1_ref, b1_ref, f1_ref, w2_ref, b2_ref, f2_ref, w3_ref, dec_ref,
                    o_ref, sum_ref, *, L, tr):
    i = pl.program_id(0)
    z = z_ref[...]
    h = jnp.sin(f1_ref[...] * (_dot_split(z, w1_ref[...]) + b1_ref[...]))
    h = jnp.sin(f2_ref[...] * (_dot_split(h, w2_ref[...]) + b2_ref[...]))
    h = _dot_split(h, w3_ref[...])
    t = z[:, 0:1]
    h = h * jnp.exp(-t * jnp.abs(dec_ref[...]))
    m = i * tr + lax.broadcasted_iota(jnp.int32, h.shape, 0)
    filt = jnp.where(m == L, 0.0, h)
    _store_tiled_pair(o_ref, (), [filt[:FFT_N2], filt[FFT_N2:]])

    @pl.when(i == 0)
    def _():
        sum_ref[...] = jnp.zeros_like(sum_ref)

    sum_ref[...] += jnp.sum(jnp.abs(filt), axis=0, keepdims=True)


def _hy_filter(zfeat, w1, b1, f1, w2, b2, f2, w3, dec, L):
    N, E = zfeat.shape
    hid = w1.shape[1]
    C4 = w3.shape[1]
    tr = 2 * FFT_N2
    nt_half = L // tr
    full = lambda a: pl.BlockSpec(a.shape, lambda i: (0,) * a.ndim)
    side = lambda a: pl.BlockSpec((a.shape[0], C4 // 2), lambda i: (0, i // nt_half))
    args = (w1, b1, f1, w2, b2, f2, w3, dec)
    return pl.pallas_call(
        functools.partial(_hy_filter_body, L=L, tr=tr),
        grid=(N // tr,),
        in_specs=[pl.BlockSpec((tr, E), lambda i: (i, 0))] + [full(a) for a in args[:6]]
        + [side(w3), side(dec)],
        out_specs=[pl.BlockSpec((NT, None, 2 * PITCH, C4 // 2), lambda i: (0, i, 0, 0)),
                   pl.BlockSpec((1, C4 // 2), lambda i: (0, 0))],
        out_shape=[jax.ShapeDtypeStruct((NT, N // tr, 2 * PITCH, C4 // 2), F32),
                   jax.ShapeDtypeStruct((1, C4 // 2), F32)],
        compiler_params=_params(("arbitrary",), 32 << 20),
        name="hy_filter",
    )(zfeat, *args)


def _stage_a_body(w_ref, x_ref, o_ref, *, nparts, rows):
    mo = w_ref.shape[0]
    w = w_ref[...]

    def body(r, carry):
        parts = [x_ref[p, pl.ds(r, rows, stride=PITCH), :] for p in range(nparts)]
        x = parts[0] if nparts == 1 else jnp.concatenate(parts, axis=0)
        o_ref[pl.ds(r, mo, stride=PITCH), :] = jnp.dot(w, x.astype(BF16), preferred_element_type=F32)
        return carry

    lax.fori_loop(0, TN2, body, 0, unroll=8)
    pad = jnp.zeros((mo, o_ref.shape[1]), F32)
    for r in range(TN2, PITCH):
        o_ref[pl.ds(r, mo, stride=PITCH), :] = pad


def _stage_a(wmat, x, group, nparts):
    mo, K = wmat.shape
    _, _, T, C = x.shape
    rows = T // PITCH
    assert K == nparts * rows
    cb = min(STAGE_CB, C)
    vmem = 2 * nparts * T * cb * 4 + 2 * mo * PITCH * cb * 4 + (12 << 20)
    return pl.pallas_call(
        functools.partial(_stage_a_body, nparts=nparts, rows=rows),
        grid=(C // cb, NT),
        in_specs=[pl.BlockSpec((mo, K), lambda j, h: (0, 0)),
                  pl.BlockSpec((nparts, None, T, cb), lambda j, h: (group, h, 0, j))],
        out_specs=pl.BlockSpec((None, mo * PITCH, cb), lambda j, h: (h, 0, j)),
        out_shape=jax.ShapeDtypeStruct((NT, mo * PITCH, C), F32),
        compiler_params=_params(("parallel", "parallel"), vmem),
        name="hy_stage_a",
    )(wmat, x)


def _hy_freq_body(g_ref, gi_ref, x_ref, f_ref, s_ref, o_ref, *, kb):
    n2 = FFT_N2
    inv = 1.0 / (s_ref[...] + 1e-6)

    def gather(ref, t):
        return jnp.concatenate([ref[h, ri, t, 0:TN2, :] for ri in range(2) for h in range(NT)], axis=0)

    pad = jnp.zeros((PITCH - TN2, o_ref.shape[-1]), F32)
    for t in range(kb):
        g = g_ref[t]
        y = jnp.dot(g, gather(x_ref, t).astype(BF16), preferred_element_type=F32)
        h = jnp.dot(g, gather(f_ref, t).astype(BF16), preferred_element_type=F32) * inv
        yr, yi = y[:n2], y[n2:]
        hr, hi = h[:n2], h[n2:]
        z = jnp.concatenate([yr * hr - yi * hi, yr * hi + yi * hr], axis=0).astype(BF16)
        e = jnp.dot(gi_ref[t], z, preferred_element_type=F32)
        for ri in range(2):
            for hh in range(NT):
                o_ref[hh, ri, t, 0:TN2, :] = e[ri * n2 + hh * TN2:ri * n2 + (hh + 1) * TN2]
                o_ref[hh, ri, t, TN2:PITCH, :] = pad


def _hy_freq(gmat, gimat, xa, fa, asum, order, C):
    N1 = xa.shape[2]
    n2 = FFT_N2
    kb = min(4, N1)
    blk = pl.BlockSpec((NT, 2, kb, PITCH, C), lambda i: (0, 0, i, 0, 0))
    gspec = pl.BlockSpec((kb, 2 * n2, 2 * n2), lambda i: (i, 0, 0))
    return pl.pallas_call(
        functools.partial(_hy_freq_body, kb=kb),
        grid=(N1 // kb,),
        in_specs=[gspec, gspec, blk,
                  pl.BlockSpec((NT, 2, kb, PITCH, C), lambda i: (0, 0, i, 0, order)),
                  pl.BlockSpec((1, C), lambda i: (0, order))],
        out_specs=blk,
        out_shape=jax.ShapeDtypeStruct(xa.shape, F32),
        compiler_params=_params(("parallel",), 6 * 2 * kb * n2 * C * 4 + 24 * n2 * C * 4 + (8 << 20)),
        name="hy_freq",
    )(gmat, gimat, xa, fa, asum)


def _stage_d_body(w_ref, e_ref, z_ref, x_ref, b_ref, o_ref, *, B, nh):
    w = w_ref[...]
    bias = b_ref[...]
    mi = w.shape[1]

    def body(r, carry):
        e = e_ref[pl.ds(r, mi, stride=PITCH), :].astype(BF16)
        conv = jnp.dot(w, e, preferred_element_type=F32)
        rows = pl.ds(r, nh, stride=PITCH)
        for b in range(B):
            o_ref[b, rows, :] = x_ref[b, rows, :] * (conv[b * nh:(b + 1) * nh] + bias * z_ref[b, rows, :])
        return carry

    lax.fori_loop(0, TN2, body, 0, unroll=8)
    pad = jnp.zeros((nh, o_ref.shape[-1]), F32)
    for r in range(TN2, PITCH):
        for b in range(B):
            o_ref[b, pl.ds(r, nh, stride=PITCH), :] = pad


def _stage_d(w4, e, z, z_group, gate, gate_group, bias, B):
    mo, mi = w4.shape
    nh = mo // B
    T, C = z.shape[2], z.shape[3]
    cb = min(STAGE_CB, C)
    nat = lambda grp: pl.BlockSpec((B, None, T, cb), lambda j, h: (grp, h, 0, j))
    vmem = 2 * mi * PITCH * cb * 4 + 6 * B * T * cb * 4 + (12 << 20)
    return pl.pallas_call(
        functools.partial(_stage_d_body, B=B, nh=nh),
        grid=(C // cb, NT),
        in_specs=[pl.BlockSpec((mo, mi), lambda j, h: (0, 0)),
                  pl.BlockSpec((None, mi * PITCH, cb), lambda j, h: (h, 0, j)),
                  nat(z_group), nat(gate_group),
                  pl.BlockSpec((1, cb), lambda j, h: (0, j))],
        out_specs=nat(0),
        out_shape=jax.ShapeDtypeStruct((B, NT, T, C), F32),
        compiler_params=_params(("parallel", "parallel"), vmem),
        name="hy_stage_d",
    )(w4, e, z, gate, bias)


def _dft_tables(S):
    N = 2 * S
    n2 = FFT_N2
    N1 = N // n2
    nh = S // n2
    k1 = np.arange(N1)
    f1 = np.exp(-2j * np.pi * np.outer(k1, np.arange(N1)) / N1)
    fa = f1[:, :nh]
    wa = np.block([[fa.real, -fa.imag], [fa.imag, fa.real]])
    wa_real = np.concatenate([f1.real, f1.imag], axis=0)
    a2 = np.arange(n2)
    base = np.exp(-2j * np.pi * np.outer(a2, a2) / n2)
    tw = np.exp(-2j * np.pi * np.outer(k1, a2) / N)
    g = base[None, :, :] * tw[:, None, :]
    gi = np.conj(np.transpose(g, (0, 2, 1)))
    blockify = lambda c: np.concatenate(
        [np.concatenate([c.real, -c.imag], axis=-1), np.concatenate([c.imag, c.real], axis=-1)], axis=-2)
    f4 = np.exp(2j * np.pi * np.outer(np.arange(nh), k1) / N1) / N
    wd = np.block([[f4.real, -f4.imag], [f4.imag, f4.real]])
    cast = lambda a: jnp.asarray(a, dtype=F32).astype(BF16)
    return dict(N1=N1, nh=nh, wa=cast(wa), wa_real=cast(wa_real), g=cast(blockify(g)),
                gi=cast(blockify(gi)), wd=cast(wd))


def _hyena(p3, col0, C, tabs, conv_w, conv_b, filt_params, hy_bias, zfeat):
    B, S, _ = p3.shape
    N1 = tabs["N1"]
    u = _hy_conv(p3, col0, 3 * C, conv_w, conv_b)
    filt, asum = _hy_filter(zfeat, *filt_params, S)
    filt = filt.reshape(1, NT, N1 * PITCH, 2 * C)
    fa = _stage_a(tabs["wa_real"], filt, 0, 1).reshape(NT, 2, N1, PITCH, 2 * C)
    z = u
    for order in range(2):
        xa = _stage_a(tabs["wa"], z, 0, B)
        e = _hy_freq(tabs["g"], tabs["gi"], xa.reshape(NT, 2, N1, PITCH, C), fa, asum, order, C)
        z = _stage_d(tabs["wd"], e.reshape(NT, 2 * N1 * PITCH, C), z, 0, u, 1 + order,
                     hy_bias[order].reshape(1, C), B)
    return z


def _scan_rows(a, b, reverse):
    R = a.shape[0]
    row = lax.broadcasted_iota(jnp.int32, a.shape, 0)
    d = 1
    while d < R:
        if reverse:
            a_s = pltpu.roll(a, R - d, axis=0)
            b_s = pltpu.roll(b, R - d, axis=0)
            m = row < R - d
        else:
            a_s = pltpu.roll(a, d, axis=0)
            b_s = pltpu.roll(b, d, axis=0)
            m = row >= d
        b = jnp.where(m, a * b_s + b, b)
        a = jnp.where(m, a * a_s, a)
        d *= 2
    return a, b


def _gelu_tanh(x):
    return 0.5 * x * (1.0 + jnp.tanh(0.7978845608028654 * (x + 0.044715 * x * x * x)))


def _lru_body(g_ref, x_ref, cw_ref, cb_ref, w_ref, bias_ref, lam_ref, o_ref,
              xs_ref, hf_ref, ab_ref, bb_ref, *, S, R, K):
    bd = x_ref.shape[1]
    _fill_haloed(xs_ref, x_ref, S)
    taps = [cw_ref[j:j + 1, :] for j in range(K)]
    cbias = cb_ref[...]
    w = w_ref[...]
    bias = bias_ref[...]
    nl = -lam_ref[...]
    sp = jnp.maximum(nl, 0.0) + jnp.log1p(jnp.exp(-jnp.abs(nl)))
    sp0 = sp[0:1, :]
    sp1 = sp[1:2, :]
    nchunk = S // R

    def gates(r, i, spd, xr):
        log_a = -LRU_C * jax.nn.sigmoid(r) * spd
        a = jnp.exp(log_a)
        b = jnp.sqrt(1.0 - a * a) * (jax.nn.sigmoid(i) * xr)
        return a, b

    def fwd(i, hc):
        r0 = pl.multiple_of(i * R, R)
        xr = _conv_rows(xs_ref, r0, R, taps, (K - 1) // 2, cbias)
        proj = jnp.dot(xr.astype(BF16), w, preferred_element_type=F32) + bias
        a0, b0 = gates(proj[:, 0:bd], proj[:, bd:2 * bd], sp0, xr)
        a1, b1 = gates(proj[:, 2 * bd:3 * bd], proj[:, 3 * bd:4 * bd], sp1, xr)
        ab_ref[pl.ds(r0, R), :] = a1
        bb_ref[pl.ds(r0, R), :] = b1
        ac, hl = _scan_rows(a0, b0, reverse=False)
        h = ac * hc + hl
        hf_ref[pl.ds(r0, R), :] = h
        return h[R - 1:R, :]

    lax.fori_loop(0, nchunk, fwd, jnp.zeros((1, bd), F32))

    def bwd(t, hc):
        r0 = pl.multiple_of((nchunk - 1 - t) * R, R)
        sl = pl.ds(r0, R)
        ac, hl = _scan_rows(ab_ref[sl, :], bb_ref[sl, :], reverse=True)
        h = ac * hc + hl
        o_ref[sl, :] = _gelu_tanh(g_ref[sl, :]) * (hf_ref[sl, :] + h)
        return h[0:1, :]

    lax.fori_loop(0, nchunk, bwd, jnp.zeros((1, bd), F32))


def _lru(p3, col_g, col_x, width, conv_w, conv_b, wa, ba, wx, bx, lam):
    B, S, _ = p3.shape
    bd = LRU_BLOCK_DIM
    nb = width // bd
    K = conv_w.shape[0]
    R = min(256, S)
    cw = jnp.zeros((8, width), F32).at[:K].set(conv_w)
    wcat = jnp.concatenate([wa[0], wx[0], wa[1], wx[1]], axis=-1).astype(BF16)
    bsplit = lambda v: v.reshape(nb, 1, bd)
    bcat = jnp.concatenate([bsplit(ba[0]), bsplit(bx[0]), bsplit(ba[1]), bsplit(bx[1])], axis=-1)
    lam8 = jnp.zeros((8, width), F32).at[:2].set(lam)
    og, ox = col_g // bd, col_x // bd
    return pl.pallas_call(
        functools.partial(_lru_body, S=S, R=R, K=K),
        grid=(B, nb),
        in_specs=[
            pl.BlockSpec((None, S, bd), lambda b, j: (b, 0, og + j)),
            pl.BlockSpec((None, S, bd), lambda b, j: (b, 0, ox + j)),
            pl.BlockSpec((8, bd), lambda b, j: (0, j)),
            pl.BlockSpec((1, bd), lambda b, j: (0, j)),
            pl.BlockSpec((None, bd, 4 * bd), lambda b, j: (j, 0, 0)),
            pl.BlockSpec((None, 1, 4 * bd), lambda b, j: (j, 0, 0)),
            pl.BlockSpec((8, bd), lambda b, j: (0, j)),
        ],
        out_specs=pl.BlockSpec((None, S, bd), lambda b, j: (b, 0, j)),
        out_shape=jax.ShapeDtypeStruct((B, S, width), F32),
        scratch_shapes=[pltpu.VMEM((S + 2 * HALO, bd), F32), pltpu.VMEM((S, bd), F32),
                        pltpu.VMEM((S, bd), F32), pltpu.VMEM((S, bd), F32)],
        compiler_params=_params(("parallel", "parallel"), 10 * S * bd * 4 + (12 << 20)),
        name="lru",
    )(p3, p3, cw, conv_b.reshape(1, width), wcat, bcat, lam8)


def _mixnorm_body(a_ref, b_ref, c_ref, g_ref, o_ref):
    off = 0
    for ref in (a_ref, b_ref, c_ref):
        if len(ref.shape) == 3:
            y = jnp.concatenate([ref[h, 0:TN2, :] for h in range(NT)], axis=0)
        else:
            y = ref[...]
        w = y.shape[1]
        ms = jnp.mean(y * y, axis=-1, keepdims=True)
        o_ref[:, off:off + w] = (y * lax.rsqrt(ms + 1e-6) * g_ref[:, off:off + w]).astype(BF16)
        off += w


def _mixnorm(ya, yb_tiled, yc, gain):
    M = ya.shape[0]
    B, _, T, wb = yb_tiled.shape
    nh = T // PITCH
    widths = (ya.shape[1], wb, yc.shape[1])
    D = sum(widths)
    tm = FFT_N2
    nat = lambda w: pl.BlockSpec((tm, w), lambda i: (i, 0))
    tiled = pl.BlockSpec((None, NT, None, PITCH, wb), lambda i: (i // nh, 0, i % nh, 0, 0))
    return pl.pallas_call(
        _mixnorm_body,
        grid=(M // tm,),
        in_specs=[nat(widths[0]), tiled, nat(widths[2]), pl.BlockSpec((1, D), lambda i: (0, 0))],
        out_specs=pl.BlockSpec((tm, D), lambda i: (i, 0)),
        out_shape=jax.ShapeDtypeStruct((M, D), BF16),
        compiler_params=_params(("parallel",), 32 << 20),
        name="mixnorm",
    )(ya, yb_tiled.reshape(B, NT, nh, PITCH, wb), yc, gain)


def kernel(x, ffa_w_gate, ffa_w_up, ffa_w_down, ffb_w_gate, ffb_w_up, ffb_w_down, ln_gain, ln_bias,
           w_in, w_out, mix_norm_gain, hy_conv_w, hy_conv_b, hy_filt_w1, hy_filt_b1, hy_filt_freq1,
           hy_filt_w2, hy_filt_b2, hy_filt_freq2, hy_filt_w3, hy_decay, hy_bias, lru_conv_w,
           lru_conv_b, lru_wa, lru_ba, lru_wx, lru_bx, lru_lambda):
    B, S, D = x.shape
    depth = w_in.shape[0]
    M = B * S
    alpha = (2 * depth) ** 0.25
    R = D // 2
    H = R // RET_HEAD_DIM
    Hy = D // 4
    Lw = D // 4
    col_hy = 4 * R
    col_lg = col_hy + 3 * Hy
    col_lx = col_lg + Lw

    pos = jnp.arange(S, dtype=F32)
    half = RET_HEAD_DIM // 2
    inv_freq = 1.0 / (10000.0 ** jnp.linspace(0.0, 1.0, half, dtype=F32))
    ang = pos[:, None] * inv_freq[None, :]
    cos, sin = jnp.cos(ang), jnp.sin(ang)
    log_g = jnp.log1p(-jnp.exp2(-5.0 - jnp.arange(H, dtype=F32)))
    lg_tab = jnp.broadcast_to(log_g[:, None, None], (H, 8, max(512, RET_HEAD_DIM)))
    tabs = _dft_tables(S)
    m = jnp.arange(2 * S)
    fpos = jnp.where(m < S, m, 2 * S - m).astype(F32)
    emb = hy_filt_w1.shape[1]
    bands = (emb - 1) // 2
    fr = jnp.linspace(1e-4, bands - 1, bands, dtype=F32)
    wang = 2.0 * math.pi * fpos / S
    zfeat = jnp.concatenate([(fpos / max(S - 1, 1))[:, None], jnp.cos(wang[:, None] * fr),
                             -jnp.sin(wang[:, None] * fr)], axis=-1)
    epad = LANES
    zfeat = jnp.pad(zfeat, ((0, 0), (0, epad - emb)))

    assert B == 2, "the Hyena DFT carries the two batch rows as one complex signal"
    ffa = (_pack_gate_up(ffa_w_gate, ffa_w_up), _to_bf16(ffa_w_down))
    ffb = (_pack_gate_up(ffb_w_gate, ffb_w_up), _to_bf16(ffb_w_down))
    w_in_b, w_out_b = _to_bf16(w_in), _to_bf16(w_out)
    row = lambda v: v.reshape(1, -1)

    h = x.reshape(M, D)
    for l in range(depth):
        h = _ffn(h, *ffa, l, row(ln_gain[l, 0]), row(ln_bias[l, 0]), alpha)
        p3 = _inproj(h, w_in_b, l).reshape(B, S, -1)
        y_ret = _retention(p3, cos, sin, lg_tab, H)
        w1p = jnp.pad(hy_filt_w1[l], ((0, epad - emb), (0, 0)))
        filt_params = (w1p, row(hy_filt_b1[l]), row(hy_filt_freq1[l]), hy_filt_w2[l],
                       row(hy_filt_b2[l]), row(hy_filt_freq2[l]), hy_filt_w3[l], row(hy_decay[l]))
        y_hy = _hyena(p3, col_hy, Hy, tabs, hy_conv_w[l], hy_conv_b[l], filt_params, hy_bias[l], zfeat)
        y_lru = _lru(p3, col_lg, col_lx, Lw, lru_conv_w[l], lru_conv_b[l], lru_wa[l], lru_ba[l],
                     lru_wx[l], lru_bx[l], lru_lambda[l])
        yb = _mixnorm(y_ret.reshape(M, R), y_hy, y_lru.reshape(M, Lw),
                      row(mix_norm_gain[l]))
        h = _outproj(yb, w_out_b, l, h, row(ln_gain[l, 1]), row(ln_bias[l, 1]), alpha)
        h = _ffn(h, *ffb, l, row(ln_gain[l, 2]), row(ln_bias[l, 2]), alpha)
    return h.reshape(B, S, D)
```

```python
import functools
import math

import numpy as np
import jax
import jax.numpy as jnp
from jax import lax
from jax.experimental import pallas as pl
from jax.experimental.pallas import tpu as pltpu

F32 = jnp.float32
BF16 = jnp.bfloat16

LANES = 128
RET_HEAD_DIM = 256
LRU_BLOCK_DIM = 128
LRU_C = 8.0
VMEM_CAP = 60 * 1024 * 1024


def _params(semantics, vmem_bytes):
    return pltpu.CompilerParams(
        dimension_semantics=semantics,
        vmem_limit_bytes=int(min(VMEM_CAP, max(vmem_bytes, 16 * 1024 * 1024))))


def _layer_norm_rows(y, g, b, eps=1e-5):
    mu = jnp.mean(y, axis=-1, keepdims=True)
    d = y - mu
    var = jnp.mean(d * d, axis=-1, keepdims=True)
    return d * lax.rsqrt(var + eps) * g + b


def _silu(x):
    return x * jax.nn.sigmoid(x)


def _ffn_body(x_ref, wgu_a, wgu_b, wd_a, wd_b, g_ref, b_ref, o_ref, xb_ref, *, alpha, rows, slab, nf):
    j = pl.program_id(1)
    nj = pl.num_programs(1)
    tm = x_ref.shape[0]
    tf = wd_a.shape[0]
    D = o_ref.shape[1]

    @pl.when(j == 0)
    def _():
        xb_ref[...] = x_ref[...].astype(BF16)
        o_ref[...] = jnp.zeros_like(o_ref)

    def hidden(wgu_ref):
        gu = jnp.dot(xb_ref[...], wgu_ref[...], preferred_element_type=F32)
        return (_silu(gu[:, :tf]) * gu[:, tf:]).astype(BF16)

    def accumulate(pairs):
        hs = [(hidden(wgu_ref), wd_ref) for wgu_ref, wd_ref in pairs]
        for c0 in range(0, D, slab):
            part = None
            for h, wd_ref in hs:
                p = jnp.dot(h, wd_ref[:, c0:c0 + slab], preferred_element_type=F32)
                part = p if part is None else part + p
            o_ref[:, c0:c0 + slab] += part

    if nf % 2 == 0:
        accumulate([(wgu_a, wd_a), (wgu_b, wd_b)])
    else:
        @pl.when(2 * j + 1 < nf)
        def _():
            accumulate([(wgu_a, wd_a), (wgu_b, wd_b)])

        @pl.when(2 * j + 1 >= nf)
        def _():
            accumulate([(wgu_a, wd_a)])

    @pl.when(j == nj - 1)
    def _():
        g = g_ref[...]
        b = b_ref[...]

        def body(r, carry):
            sl = pl.ds(pl.multiple_of(r * rows, rows), rows)
            y = alpha * x_ref[sl, :] + 0.5 * o_ref[sl, :]
            o_ref[sl, :] = _layer_norm_rows(y, g, b)
            return carry

        lax.fori_loop(0, tm // rows, body, 0, unroll=2)


def _cast_body(x_ref, o_ref):
    o_ref[...] = x_ref[...].astype(BF16)


def _to_bf16(w):
    L, R, C = w.shape
    tr = 256
    while tr > 8 and (R % tr or tr * C * 4 > (12 << 20)):
        tr //= 2
    spec = pl.BlockSpec((None, tr, C), lambda l, i: (l, i, 0))
    return pl.pallas_call(
        _cast_body,
        grid=(L, R // tr),
        in_specs=[spec],
        out_specs=spec,
        out_shape=jax.ShapeDtypeStruct(w.shape, BF16),
        compiler_params=_params(("parallel", "parallel"), 3 * tr * C * 4 + (8 << 20)),
        name="to_bf16",
    )(w)


FFN_TF = 256


def _pack_gate_up_body(g_ref, u_ref, o_ref, *, tf):
    for c in range(g_ref.shape[1] // tf):
        o_ref[c, :, :tf] = g_ref[:, c * tf:(c + 1) * tf].astype(BF16)
        o_ref[c, :, tf:] = u_ref[:, c * tf:(c + 1) * tf].astype(BF16)


def _pack_gate_up(wg, wu):
    L, D, F = wg.shape
    tf = FFN_TF if F % FFN_TF == 0 else F
    tr = min(128, D)
    spec = pl.BlockSpec((None, tr, F), lambda l, i: (l, i, 0))
    return pl.pallas_call(
        functools.partial(_pack_gate_up_body, tf=tf),
        grid=(L, D // tr),
        in_specs=[spec, spec],
        out_specs=pl.BlockSpec((None, F // tf, tr, 2 * tf), lambda l, i: (l, 0, i, 0)),
        out_shape=jax.ShapeDtypeStruct((L, F // tf, D, 2 * tf), BF16),
        compiler_params=_params(("parallel", "parallel"), 6 * tr * F * 4 + (8 << 20)),
        name="pack_gate_up",
    )(wg, wu)


def _ffn(x, wgu, wd, layer, g, b, alpha):
    M, D = x.shape
    F = wd.shape[1]
    tm = min(512, M)
    tf = FFN_TF if F % FFN_TF == 0 else F
    nf = F // tf
    rows = min(16, tm)
    slab = 1024 if D % 1024 == 0 else D
    vmem = (tm * D * 4 + tm * D * 2 + 2 * tm * D * 4 + 2 * 2 * 3 * D * tf * 2 + 2 * 6 * tm * tf * 4
            + 2 * tm * slab * 4)
    blk_a = lambda i, j: 2 * j
    blk_b = lambda i, j: jnp.minimum(2 * j + 1, nf - 1)
    wgu_spec = lambda blk: pl.BlockSpec((None, None, D, 2 * tf), lambda i, j: (layer, blk(i, j), 0, 0))
    wd_spec = lambda blk: pl.BlockSpec((None, tf, D), lambda i, j: (layer, blk(i, j), 0))
    return pl.pallas_call(
        functools.partial(_ffn_body, alpha=alpha, rows=rows, slab=slab, nf=nf),
        grid=(M // tm, (nf + 1) // 2),
        in_specs=[
            pl.BlockSpec((tm, D), lambda i, j: (i, 0), pipeline_mode=pl.Buffered(1)),
            wgu_spec(blk_a), wgu_spec(blk_b), wd_spec(blk_a), wd_spec(blk_b),
            pl.BlockSpec((1, D), lambda i, j: (0, 0)),
            pl.BlockSpec((1, D), lambda i, j: (0, 0)),
        ],
        out_specs=pl.BlockSpec((tm, D), lambda i, j: (i, 0)),
        out_shape=jax.ShapeDtypeStruct((M, D), F32),
        scratch_shapes=[pltpu.VMEM((tm, D), BF16)],
        compiler_params=_params(("parallel", "arbitrary"), vmem + (2 << 20)),
        name="ffn",
    )(x, wgu, wgu, wd, wd, g, b)


def _inproj_body(x_ref, w_ref, o_ref, xb_ref):
    @pl.when(pl.program_id(1) == 0)
    def _():
        xb_ref[...] = x_ref[...].astype(BF16)

    o_ref[...] = jnp.dot(xb_ref[...], w_ref[...], preferred_element_type=F32)


def _inproj(x, w, layer):
    M, K = x.shape
    N = w.shape[2]
    tm = min(512, M)
    tn = 1024 if N % 1024 == 0 else (512 if N % 512 == 0 else 128)
    vmem = 2 * tm * K * 4 + tm * K * 2 + 2 * K * tn * 2 + 2 * tm * tn * 4
    return pl.pallas_call(
        _inproj_body,
        grid=(M // tm, N // tn),
        in_specs=[
            pl.BlockSpec((tm, K), lambda i, j: (i, 0)),
            pl.BlockSpec((None, K, tn), lambda i, j: (layer, 0, j)),
        ],
        out_specs=pl.BlockSpec((tm, tn), lambda i, j: (i, j)),
        out_shape=jax.ShapeDtypeStruct((M, N), F32),
        scratch_shapes=[pltpu.VMEM((tm, K), BF16)],
        compiler_params=_params(("parallel", "arbitrary"), vmem + (4 << 20)),
        name="inproj",
    )(x, w)


def _outproj_body(y_ref, w_ref, x_ref, g_ref, b_ref, o_ref, *, alpha, rows, nj, slab):
    j = pl.program_id(1)
    tm, D = o_ref.shape
    rc = x_ref.shape[0]

    @pl.when(j == 0)
    def _():
        o_ref[...] = jnp.zeros_like(o_ref)

    yk = y_ref[...]
    for c0 in range(0, D, slab):
        o_ref[:, c0:c0 + slab] += jnp.dot(yk, w_ref[:, c0:c0 + slab], preferred_element_type=F32)
    sl = pl.ds(pl.multiple_of(j * rc, rc), rc)
    o_ref[sl, :] += alpha * x_ref[...]

    @pl.when(j == nj - 1)
    def _():
        g = g_ref[...]
        b = b_ref[...]

        def body(r, carry):
            rs = pl.ds(pl.multiple_of(r * rows, rows), rows)
            o_ref[rs, :] = _layer_norm_rows(o_ref[rs, :], g, b)
            return carry

        lax.fori_loop(0, tm // rows, body, 0, unroll=2)


def _outproj(y, w, layer, x, g, b, alpha):
    M, K = y.shape
    D = w.shape[2]
    tm = min(512, M)
    tk = 1024 if K % 1024 == 0 else K
    nj = K // tk
    rc = tm // nj
    rows = min(16, tm)
    slab = 1024 if D % 1024 == 0 else D
    vmem = 2 * tm * tk * 2 + 2 * tk * D * 2 + 2 * rc * D * 4 + 2 * tm * D * 4 + 2 * tm * slab * 4
    return pl.pallas_call(
        functools.partial(_outproj_body, alpha=alpha, rows=rows, nj=nj, slab=slab),
        grid=(M // tm, nj),
        in_specs=[
            pl.BlockSpec((tm, tk), lambda i, j: (i, j)),
            pl.BlockSpec((None, tk, D), lambda i, j: (layer, j, 0)),
            pl.BlockSpec((rc, D), lambda i, j: (i * nj + j, 0)),
            pl.BlockSpec((1, D), lambda i, j: (0, 0)),
            pl.BlockSpec((1, D), lambda i, j: (0, 0)),
        ],
        out_specs=pl.BlockSpec((tm, D), lambda i, j: (i, 0)),
        out_shape=jax.ShapeDtypeStruct((M, D), F32),
        compiler_params=_params(("parallel", "arbitrary"), vmem + (6 << 20)),
        name="outproj",
    )(y, w, x, g, b)


def _rotary(x, cos, sin):
    half = x.shape[-1] // 2
    x1 = x[:, :half]
    x2 = x[:, half:]
    return jnp.concatenate([x1 * cos - x2 * sin, x1 * sin + x2 * cos], axis=-1)


def _ret_qkv(q_ref, k_ref, cos_ref, sin_ref, chunk):
    C = q_ref.shape[0]
    rows = pl.ds(pl.multiple_of(chunk * C, C), C)
    cos = cos_ref[rows, :]
    sin = sin_ref[rows, :]
    q = _rotary(q_ref[...], cos, sin)
    k = _rotary(k_ref[...], cos, sin) * (RET_HEAD_DIM ** -0.5)
    return q, k


def _row_index(shape):
    return lax.broadcasted_iota(jnp.int32, shape, 0).astype(F32)


def _ret_fwd_body(q_ref, k_ref, v_ref, cos_ref, sin_ref, lg_ref, y_ref,
                  state_ref, qd_ref, kd_ref, intra_ref):
    C, dk = q_ref.shape
    lg = lg_ref[0:1, :dk]

    @pl.when(pl.program_id(2) == 0)
    def _():
        state_ref[...] = jnp.zeros_like(state_ref)
        row = _row_index((C, dk))
        qd_ref[...] = jnp.exp(lg * (row + 1.0))
        kd_ref[...] = jnp.exp(lg * (C - 1.0 - row))
        ii = lax.broadcasted_iota(jnp.int32, (C, C), 0)
        jj = lax.broadcasted_iota(jnp.int32, (C, C), 1)
        intra_ref[...] = jnp.exp(lg_ref[0:1, :C] * jnp.abs(ii - jj).astype(F32))

    q, k = _ret_qkv(q_ref, k_ref, cos_ref, sin_ref, pl.program_id(2))
    vb = v_ref[...].astype(BF16)
    q_in = (q * qd_ref[...]).astype(BF16)
    k_out = (k * kd_ref[...]).astype(BF16)
    s = lax.dot_general(q.astype(BF16), k.astype(BF16), (((1,), (1,)), ((), ())),
                        preferred_element_type=F32) * intra_ref[...]
    state = state_ref[...]
    y = jnp.dot(s.astype(BF16), vb, preferred_element_type=F32)
    y = y + jnp.dot(q_in, state.astype(BF16), preferred_element_type=F32)
    y_ref[...] = y
    kv = lax.dot_general(k_out, vb, (((0,), (0,)), ((), ())), preferred_element_type=F32)
    state_ref[...] = state * jnp.exp(lg * float(C)) + kv


def _ret_bwd_body(q_ref, k_ref, v_ref, g_ref, cos_ref, sin_ref, lg_ref, yf_ref, y_ref,
                  state_ref, qd_ref, kd_ref):
    C, dk = q_ref.shape
    lg = lg_ref[0:1, :dk]

    @pl.when(pl.program_id(2) == 0)
    def _():
        state_ref[...] = jnp.zeros_like(state_ref)
        row = _row_index((C, dk))
        qd_ref[...] = jnp.exp(lg * (C - row))
        kd_ref[...] = jnp.exp(lg * row)

    q, k = _ret_qkv(q_ref, k_ref, cos_ref, sin_ref, pl.num_programs(2) - 1 - pl.program_id(2))
    vb = v_ref[...].astype(BF16)
    q_in = (q * qd_ref[...]).astype(BF16)
    k_out = (k * kd_ref[...]).astype(BF16)
    state = state_ref[...]
    y = yf_ref[...] + jnp.dot(q_in, state.astype(BF16), preferred_element_type=F32)
    kv = lax.dot_general(k_out, vb, (((0,), (0,)), ((), ())), preferred_element_type=F32)
    state_ref[...] = state * jnp.exp(lg * float(C)) + kv
    mu = jnp.mean(y, axis=-1, keepdims=True)
    d = y - mu
    var = jnp.mean(d * d, axis=-1, keepdims=True)
    y_ref[...] = _silu(g_ref[...]) * (d * lax.rsqrt(var + 1e-5))


def _retention(p3, cos, sin, lg_tab, H):
    B, S, _ = p3.shape
    dk = RET_HEAD_DIM
    C = min(512, S)
    nc = S // C
    W = lg_tab.shape[-1]
    blk = lambda off, rev: pl.BlockSpec(
        (None, C, dk),
        (lambda b, h, c: (b, nc - 1 - c, off + h)) if rev else (lambda b, h, c: (b, c, off + h)))
    tab = lambda rev: pl.BlockSpec((S, dk // 2), lambda b, h, c: (0, 0))
    lg_spec = pl.BlockSpec((None, 8, W), lambda b, h, c: (h, 0, 0))
    vmem = 32 << 20
    y_f = pl.pallas_call(
        _ret_fwd_body,
        grid=(B, H, nc),
        in_specs=[blk(0, False), blk(H, False), blk(2 * H, False), tab(False), tab(False), lg_spec],
        out_specs=pl.BlockSpec((None, C, dk), lambda b, h, c: (b, c, h)),
        out_shape=jax.ShapeDtypeStruct((B, S, H * dk), F32),
        scratch_shapes=[pltpu.VMEM((dk, dk), F32), pltpu.VMEM((C, dk), F32), pltpu.VMEM((C, dk), F32),
                        pltpu.VMEM((C, C), F32)],
        compiler_params=_params(("parallel", "parallel", "arbitrary"), vmem),
        name="ret_fwd",
    )(p3, p3, p3, cos, sin, lg_tab)
    return pl.pallas_call(
        _ret_bwd_body,
        grid=(B, H, nc),
        in_specs=[blk(0, True), blk(H, True), blk(2 * H, True), blk(3 * H, True),
                  tab(True), tab(True), lg_spec,
                  pl.BlockSpec((None, C, dk), lambda b, h, c: (b, nc - 1 - c, h))],
        out_specs=pl.BlockSpec((None, C, dk), lambda b, h, c: (b, nc - 1 - c, h)),
        out_shape=jax.ShapeDtypeStruct((B, S, H * dk), F32),
        scratch_shapes=[pltpu.VMEM((dk, dk), F32), pltpu.VMEM((C, dk), F32), pltpu.VMEM((C, dk), F32)],
        compiler_params=_params(("parallel", "parallel", "arbitrary"), vmem),
        name="ret_bwd",
    )(p3, p3, p3, p3, cos, sin, lg_tab, y_f)


HALO = 8


def _fill_haloed(xs_ref, x_ref, S):
    zeros = jnp.zeros((HALO, xs_ref.shape[1]), F32)
    xs_ref[0:HALO, :] = zeros
    xs_ref[S + HALO:S + 2 * HALO, :] = zeros
    xs_ref[HALO:S + HALO, :] = x_ref[...]


def _conv_rows(xs_ref, r0, R, taps, left, bias):
    ext = xs_ref[pl.ds(r0, R + 2 * HALO), :]
    n = R + 2 * HALO
    acc = None
    for j, w in enumerate(taps):
        d = j - left
        sh = ext if d == 0 else pltpu.roll(ext, (-d) % n, axis=0)
        term = w * sh[HALO:HALO + R, :]
        acc = term if acc is None else acc + term
    return acc + bias


FFT_N2 = 128
TN2 = 32
NT = FFT_N2 // TN2
PITCH = TN2 + 4
STAGE_CB = LANES


def _store_tiled_pair(o_ref, lead, blocks):
    pad = jnp.zeros((PITCH - TN2, blocks[0].shape[1]), F32)
    for h in range(NT):
        for q, block in enumerate(blocks):
            o_ref[(h, *lead, slice(q * PITCH, q * PITCH + TN2))] = block[h * TN2:(h + 1) * TN2, :]
            o_ref[(h, *lead, slice(q * PITCH + TN2, (q + 1) * PITCH))] = pad


def _hy_conv_body(u_ref, w_ref, b_ref, o_ref, xs_ref, *, S, R, K):
    _fill_haloed(xs_ref, u_ref, S)
    taps = [w_ref[j:j + 1, :] for j in range(K)]
    bias = b_ref[...]

    def body(i, carry):
        r0 = pl.multiple_of(i * R, R)
        res = _conv_rows(xs_ref, r0, R, taps, (K - 1) // 2, bias)
        _store_tiled_pair(o_ref, (i,), [res[:FFT_N2], res[FFT_N2:]])
        return carry

    lax.fori_loop(0, S // R, body, 0)


def _hy_conv(p3, col0, width, w, b):
    B, S, _ = p3.shape
    K = w.shape[0]
    cb = LANES
    nb = width // cb
    C = width // 3
    R = 2 * FFT_N2
    npair = S // R
    wp = jnp.zeros((8, width), F32).at[:K].set(w)
    off = col0 // cb
    per = C // cb
    return pl.pallas_call(
        functools.partial(_hy_conv_body, S=S, R=R, K=K),
        grid=(B, nb),
        in_specs=[
            pl.BlockSpec((None, S, cb), lambda b_, j: (b_, 0, off + j)),
            pl.BlockSpec((8, cb), lambda b_, j: (0, j)),
            pl.BlockSpec((1, cb), lambda b_, j: (0, j)),
        ],
        out_specs=pl.BlockSpec((None, NT, npair, 2 * PITCH, cb),
                               lambda b_, j: ((j // per) * B + b_, 0, 0, 0, j % per)),
        out_shape=jax.ShapeDtypeStruct((3 * B, NT, npair, 2 * PITCH, C), F32),
        scratch_shapes=[pltpu.VMEM((S + 2 * HALO, cb), F32)],
        compiler_params=_params(("parallel", "parallel"), 7 * S * cb * 4 + (8 << 20)),
        name="hy_conv",
    )(p3, wp, b.reshape(1, width)).reshape(3 * B, NT, npair * 2 * PITCH, C)


def _dot_split(a, b):
    a_hi = a.astype(BF16).astype(F32)
    b_hi = b.astype(BF16).astype(F32)
    lhs = jnp.concatenate([a_hi, a_hi, a - a_hi], axis=1).astype(BF16)
    rhs = jnp.concatenate([b_hi, b - b_hi, b_hi], axis=0).astype(BF16)
    return jnp.dot(lhs, rhs, preferred_element_type=F32)


def _hy_filter_body(z_ref, w1_ref, b1_ref, f1_ref, w2_ref, b2_ref, f2_ref, w3_ref, dec_ref,
                    o_ref, sum_ref, *, L, tr):
    i = pl.program_id(0)
    z = z_ref[...]
    h = jnp.sin(f1_ref[...] * (_dot_split(z, w1_ref[...]) + b1_ref[...]))
    h = jnp.sin(f2_ref[...] * (_dot_split(h, w2_ref[...]) + b2_ref[...]))
    h = _dot_split(h, w3_ref[...])
    t = z[:, 0:1]
    h = h * jnp.exp(-t * jnp.abs(dec_ref[...]))
    m = i * tr + lax.broadcasted_iota(jnp.int32, h.shape, 0)
    filt = jnp.where(m == L, 0.0, h)
    _store_tiled_pair(o_ref, (), [filt[:FFT_N2], filt[FFT_N2:]])

    @pl.when(i == 0)
    def _():
        sum_ref[...] = jnp.zeros_like(sum_ref)

    sum_ref[...] += jnp.sum(jnp.abs(filt), axis=0, keepdims=True)


def _hy_filter(zfeat, w1, b1, f1, w2, b2, f2, w3, dec, L):
    N, E = zfeat.shape
    hid = w1.shape[1]
    C4 = w3.shape[1]
    tr = 2 * FFT_N2
    nt_half = L // tr
    full = lambda a: pl.BlockSpec(a.shape, lambda i: (0,) * a.ndim)
    side = lambda a: pl.BlockSpec((a.shape[0], C4 // 2), lambda i: (0, i // nt_half))
    args = (w1, b1, f1, w2, b2, f2, w3, dec)
    return pl.pallas_call(
        functools.partial(_hy_filter_body, L=L, tr=tr),
        grid=(N // tr,),
        in_specs=[pl.BlockSpec((tr, E), lambda i: (i, 0))] + [full(a) for a in args[:6]]
        + [side(w3), side(dec)],
        out_specs=[pl.BlockSpec((NT, None, 2 * PITCH, C4 // 2), lambda i: (0, i, 0, 0)),
                   pl.BlockSpec((1, C4 // 2), lambda i: (0, 0))],
        out_shape=[jax.ShapeDtypeStruct((NT, N // tr, 2 * PITCH, C4 // 2), F32),
                   jax.ShapeDtypeStruct((1, C4 // 2), F32)],
        compiler_params=_params(("arbitrary",), 32 << 20),
        name="hy_filter",
    )(zfeat, *args)


def _stage_a_body(w_ref, x_ref, o_ref, *, nparts, rows):
    mo = w_ref.shape[0]
    w = w_ref[...]

    def body(r, carry):
        parts = [x_ref[p, pl.ds(r, rows, stride=PITCH), :] for p in range(nparts)]
        x = parts[0] if nparts == 1 else jnp.concatenate(parts, axis=0)
        o_ref[pl.ds(r, mo, stride=PITCH), :] = jnp.dot(w, x.astype(BF16), preferred_element_type=F32)
        return carry

    lax.fori_loop(0, TN2, body, 0, unroll=8)
    pad = jnp.zeros((mo, o_ref.shape[1]), F32)
    for r in range(TN2, PITCH):
        o_ref[pl.ds(r, mo, stride=PITCH), :] = pad


def _stage_a(wmat, x, group, nparts):
    mo, K = wmat.shape
    _, _, T, C = x.shape
    rows = T // PITCH
    assert K == nparts * rows
    cb = min(STAGE_CB, C)
    vmem = 2 * nparts * T * cb * 4 + 2 * mo * PITCH * cb * 4 + (12 << 20)
    return pl.pallas_call(
        functools.partial(_stage_a_body, nparts=nparts, rows=rows),
        grid=(C // cb, NT),
        in_specs=[pl.BlockSpec((mo, K), lambda j, h: (0, 0)),
                  pl.BlockSpec((nparts, None, T, cb), lambda j, h: (group, h, 0, j))],
        out_specs=pl.BlockSpec((None, mo * PITCH, cb), lambda j, h: (h, 0, j)),
        out_shape=jax.ShapeDtypeStruct((NT, mo * PITCH, C), F32),
        compiler_params=_params(("parallel", "parallel"), vmem),
        name="hy_stage_a",
    )(wmat, x)


def _hy_freq_body(g_ref, gi_ref, x_ref, f_ref, s_ref, o_ref, *, kb):
    n2 = FFT_N2
    inv = 1.0 / (s_ref[...] + 1e-6)

    def gather(ref, t):
        lo = (t % 2) * PITCH
        return jnp.concatenate([ref[h, ri, t // 2, lo:lo + TN2, :] for ri in range(2) for h in range(NT)],
                               axis=0)

    pad = jnp.zeros((PITCH - TN2, o_ref.shape[-1]), F32)
    for t in range(kb):
        g = g_ref[t]
        y = jnp.dot(g, gather(x_ref, t).astype(BF16), preferred_element_type=F32)
        h = jnp.dot(g, gather(f_ref, t).astype(BF16), preferred_element_type=F32) * inv
        yr, yi = y[:n2], y[n2:]
        hr, hi = h[:n2], h[n2:]
        z = jnp.concatenate([yr * hr - yi * hi, yr * hi + yi * hr], axis=0).astype(BF16)
        e = jnp.dot(gi_ref[t], z, preferred_element_type=F32)
        lo = (t % 2) * PITCH
        for ri in range(2):
            for hh in range(NT):
                o_ref[hh, ri, t // 2, lo:lo + TN2, :] = e[ri * n2 + hh * TN2:ri * n2 + (hh + 1) * TN2]
                o_ref[hh, ri, t // 2, lo + TN2:lo + PITCH, :] = pad


def _hy_freq(gmat, gimat, xa, fa, asum, order, C):
    N1 = 2 * xa.shape[2]
    n2 = FFT_N2
    kb = min(4, N1)
    blk = pl.BlockSpec((NT, 2, kb // 2, 2 * PITCH, C), lambda i: (0, 0, i, 0, 0))
    gspec = pl.BlockSpec((kb, 2 * n2, 2 * n2), lambda i: (i, 0, 0))
    return pl.pallas_call(
        functools.partial(_hy_freq_body, kb=kb),
        grid=(N1 // kb,),
        in_specs=[gspec, gspec, blk,
                  pl.BlockSpec((NT, 2, kb // 2, 2 * PITCH, C), lambda i: (0, 0, i, 0, order)),
                  pl.BlockSpec((1, C), lambda i: (0, order))],
        out_specs=blk,
        out_shape=jax.ShapeDtypeStruct(xa.shape, F32),
        compiler_params=_params(("parallel",), 6 * 2 * kb * n2 * C * 4 + 24 * n2 * C * 4 + (8 << 20)),
        name="hy_freq",
    )(gmat, gimat, xa, fa, asum)


def _stage_d_body(w_ref, e_ref, z_ref, x_ref, b_ref, o_ref, *, B, nh):
    w = w_ref[...]
    bias = b_ref[...]
    mi = w.shape[1]

    def body(r, carry):
        e = e_ref[pl.ds(r, mi, stride=PITCH), :].astype(BF16)
        conv = jnp.dot(w, e, preferred_element_type=F32)
        rows = pl.ds(r, nh, stride=PITCH)
        for b in range(B):
            o_ref[b, rows, :] = x_ref[b, rows, :] * (conv[b * nh:(b + 1) * nh] + bias * z_ref[b, rows, :])
        return carry

    lax.fori_loop(0, TN2, body, 0, unroll=8)
    pad = jnp.zeros((nh, o_ref.shape[-1]), F32)
    for r in range(TN2, PITCH):
        for b in range(B):
            o_ref[b, pl.ds(r, nh, stride=PITCH), :] = pad


def _stage_d(w4, e, z, z_group, gate, gate_group, bias, B):
    mo, mi = w4.shape
    nh = mo // B
    T, C = z.shape[2], z.shape[3]
    cb = min(STAGE_CB, C)
    nat = lambda grp: pl.BlockSpec((B, None, T, cb), lambda j, h: (grp, h, 0, j))
    vmem = 2 * mi * PITCH * cb * 4 + 6 * B * T * cb * 4 + (12 << 20)
    return pl.pallas_call(
        functools.partial(_stage_d_body, B=B, nh=nh),
        grid=(C // cb, NT),
        in_specs=[pl.BlockSpec((mo, mi), lambda j, h: (0, 0)),
                  pl.BlockSpec((None, mi * PITCH, cb), lambda j, h: (h, 0, j)),
                  nat(z_group), nat(gate_group),
                  pl.BlockSpec((1, cb), lambda j, h: (0, j))],
        out_specs=nat(0),
        out_shape=jax.ShapeDtypeStruct((B, NT, T, C), F32),
        compiler_params=_params(("parallel", "parallel"), vmem),
        name="hy_stage_d",
    )(w4, e, z, gate, bias)


def _dft_tables(S):
    N = 2 * S
    n2 = FFT_N2
    N1 = N // n2
    nh = S // n2
    k1 = np.arange(N1)
    f1 = np.exp(-2j * np.pi * np.outer(k1, np.arange(N1)) / N1)
    fa = f1[:, :nh]
    wa = np.block([[fa.real, -fa.imag], [fa.imag, fa.real]])
    wa_real = np.concatenate([f1.real, f1.imag], axis=0)
    a2 = np.arange(n2)
    base = np.exp(-2j * np.pi * np.outer(a2, a2) / n2)
    tw = np.exp(-2j * np.pi * np.outer(k1, a2) / N)
    g = base[None, :, :] * tw[:, None, :]
    gi = np.conj(np.transpose(g, (0, 2, 1)))
    blockify = lambda c: np.concatenate(
        [np.concatenate([c.real, -c.imag], axis=-1), np.concatenate([c.imag, c.real], axis=-1)], axis=-2)
    f4 = np.exp(2j * np.pi * np.outer(np.arange(nh), k1) / N1) / N
    wd = np.block([[f4.real, -f4.imag], [f4.imag, f4.real]])
    cast = lambda a: jnp.asarray(a, dtype=F32).astype(BF16)
    return dict(N1=N1, nh=nh, wa=cast(wa), wa_real=cast(wa_real), g=cast(blockify(g)),
                gi=cast(blockify(gi)), wd=cast(wd))


def _hyena(p3, col0, C, tabs, conv_w, conv_b, filt_params, hy_bias, zfeat):
    B, S, _ = p3.shape
    N1 = tabs["N1"]
    u = _hy_conv(p3, col0, 3 * C, conv_w, conv_b)
    filt, asum = _hy_filter(zfeat, *filt_params, S)
    filt = filt.reshape(1, NT, N1 * PITCH, 2 * C)
    fa = _stage_a(tabs["wa_real"], filt, 0, 1).reshape(NT, 2, N1 // 2, 2 * PITCH, 2 * C)
    z = u
    for order in range(2):
        xa = _stage_a(tabs["wa"], z, 0, B)
        e = _hy_freq(tabs["g"], tabs["gi"], xa.reshape(NT, 2, N1 // 2, 2 * PITCH, C), fa, asum, order, C)
        z = _stage_d(tabs["wd"], e.reshape(NT, 2 * N1 * PITCH, C), z, 0, u, 1 + order,
                     hy_bias[order].reshape(1, C), B)
    return z


def _scan_rows(a, b, reverse):
    R = a.shape[0]
    row = lax.broadcasted_iota(jnp.int32, a.shape, 0)
    d = 1
    while d < R:
        if reverse:
            a_s = pltpu.roll(a, R - d, axis=0)
            b_s = pltpu.roll(b, R - d, axis=0)
            m = row < R - d
        else:
            a_s = pltpu.roll(a, d, axis=0)
            b_s = pltpu.roll(b, d, axis=0)
            m = row >= d
        b = jnp.where(m, a * b_s + b, b)
        a = jnp.where(m, a * a_s, a)
        d *= 2
    return a, b


def _gelu_tanh(x):
    return 0.5 * x * (1.0 + jnp.tanh(0.7978845608028654 * (x + 0.044715 * x * x * x)))


def _lru_body(g_ref, x_ref, cw_ref, cb_ref, w_ref, bias_ref, lam_ref, o_ref,
              xs_ref, hf_ref, ab_ref, bb_ref, *, S, R, K):
    bd = x_ref.shape[1]
    _fill_haloed(xs_ref, x_ref, S)
    taps = [cw_ref[j:j + 1, :] for j in range(K)]
    cbias = cb_ref[...]
    w = w_ref[...]
    bias = bias_ref[...]
    nl = -lam_ref[...]
    sp = jnp.maximum(nl, 0.0) + jnp.log1p(jnp.exp(-jnp.abs(nl)))
    sp0 = sp[0:1, :]
    sp1 = sp[1:2, :]
    nchunk = S // R

    def gates(r, i, spd, xr):
        log_a = -LRU_C * jax.nn.sigmoid(r) * spd
        a = jnp.exp(log_a)
        b = jnp.sqrt(1.0 - a * a) * (jax.nn.sigmoid(i) * xr)
        return a, b

    def fwd(i, hc):
        r0 = pl.multiple_of(i * R, R)
        xr = _conv_rows(xs_ref, r0, R, taps, (K - 1) // 2, cbias)
        proj = jnp.dot(xr.astype(BF16), w, preferred_element_type=F32) + bias
        a0, b0 = gates(proj[:, 0:bd], proj[:, bd:2 * bd], sp0, xr)
        a1, b1 = gates(proj[:, 2 * bd:3 * bd], proj[:, 3 * bd:4 * bd], sp1, xr)
        ab_ref[pl.ds(r0, R), :] = a1
        bb_ref[pl.ds(r0, R), :] = b1
        ac, hl = _scan_rows(a0, b0, reverse=False)
        h = ac * hc + hl
        hf_ref[pl.ds(r0, R), :] = h
        return h[R - 1:R, :]

    lax.fori_loop(0, nchunk, fwd, jnp.zeros((1, bd), F32))

    def bwd(t, hc):
        r0 = pl.multiple_of((nchunk - 1 - t) * R, R)
        sl = pl.ds(r0, R)
        ac, hl = _scan_rows(ab_ref[sl, :], bb_ref[sl, :], reverse=True)
        h = ac * hc + hl
        o_ref[sl, :] = _gelu_tanh(g_ref[sl, :]) * (hf_ref[sl, :] + h)
        return h[0:1, :]

    lax.fori_loop(0, nchunk, bwd, jnp.zeros((1, bd), F32))


def _lru(p3, col_g, col_x, width, conv_w, conv_b, wa, ba, wx, bx, lam):
    B, S, _ = p3.shape
    bd = LRU_BLOCK_DIM
    nb = width // bd
    K = conv_w.shape[0]
    R = min(256, S)
    cw = jnp.zeros((8, width), F32).at[:K].set(conv_w)
    wcat = jnp.concatenate([wa[0], wx[0], wa[1], wx[1]], axis=-1).astype(BF16)
    bsplit = lambda v: v.reshape(nb, 1, bd)
    bcat = jnp.concatenate([bsplit(ba[0]), bsplit(bx[0]), bsplit(ba[1]), bsplit(bx[1])], axis=-1)
    lam8 = jnp.zeros((8, width), F32).at[:2].set(lam)
    og, ox = col_g // bd, col_x // bd
    return pl.pallas_call(
        functools.partial(_lru_body, S=S, R=R, K=K),
        grid=(B, nb),
        in_specs=[
            pl.BlockSpec((None, S, bd), lambda b, j: (b, 0, og + j)),
            pl.BlockSpec((None, S, bd), lambda b, j: (b, 0, ox + j)),
            pl.BlockSpec((8, bd), lambda b, j: (0, j)),
            pl.BlockSpec((1, bd), lambda b, j: (0, j)),
            pl.BlockSpec((None, bd, 4 * bd), lambda b, j: (j, 0, 0)),
            pl.BlockSpec((None, 1, 4 * bd), lambda b, j: (j, 0, 0)),
            pl.BlockSpec((8, bd), lambda b, j: (0, j)),
        ],
        out_specs=pl.BlockSpec((None, S, bd), lambda b, j: (b, 0, j)),
        out_shape=jax.ShapeDtypeStruct((B, S, width), F32),
        scratch_shapes=[pltpu.VMEM((S + 2 * HALO, bd), F32), pltpu.VMEM((S, bd), F32),
                        pltpu.VMEM((S, bd), F32), pltpu.VMEM((S, bd), F32)],
        compiler_params=_params(("parallel", "parallel"), 10 * S * bd * 4 + (12 << 20)),
        name="lru",
    )(p3, p3, cw, conv_b.reshape(1, width), wcat, bcat, lam8)


def _mixnorm_body(a_ref, b_ref, c_ref, g_ref, o_ref):
    off = 0
    for ref in (a_ref, b_ref, c_ref):
        if len(ref.shape) == 3:
            y = jnp.concatenate([ref[h, q * PITCH:q * PITCH + TN2, :] for q in range(2) for h in range(NT)],
                                axis=0)
        else:
            y = ref[...]
        w = y.shape[1]
        ms = jnp.mean(y * y, axis=-1, keepdims=True)
        o_ref[:, off:off + w] = (y * lax.rsqrt(ms + 1e-6) * g_ref[:, off:off + w]).astype(BF16)
        off += w


def _mixnorm(ya, yb_tiled, yc, gain):
    M = ya.shape[0]
    B, _, T, wb = yb_tiled.shape
    npair = T // (2 * PITCH)
    widths = (ya.shape[1], wb, yc.shape[1])
    D = sum(widths)
    tm = 2 * FFT_N2
    nat = lambda w: pl.BlockSpec((tm, w), lambda i: (i, 0))
    tiled = pl.BlockSpec((None, NT, None, 2 * PITCH, wb), lambda i: (i // npair, 0, i % npair, 0, 0))
    return pl.pallas_call(
        _mixnorm_body,
        grid=(M // tm,),
        in_specs=[nat(widths[0]), tiled, nat(widths[2]), pl.BlockSpec((1, D), lambda i: (0, 0))],
        out_specs=pl.BlockSpec((tm, D), lambda i: (i, 0)),
        out_shape=jax.ShapeDtypeStruct((M, D), BF16),
        compiler_params=_params(("parallel",), 32 << 20),
        name="mixnorm",
    )(ya, yb_tiled.reshape(B, NT, npair, 2 * PITCH, wb), yc, gain)


def kernel(x, ffa_w_gate, ffa_w_up, ffa_w_down, ffb_w_gate, ffb_w_up, ffb_w_down, ln_gain, ln_bias,
           w_in, w_out, mix_norm_gain, hy_conv_w, hy_conv_b, hy_filt_w1, hy_filt_b1, hy_filt_freq1,
           hy_filt_w2, hy_filt_b2, hy_filt_freq2, hy_filt_w3, hy_decay, hy_bias, lru_conv_w,
           lru_conv_b, lru_wa, lru_ba, lru_wx, lru_bx, lru_lambda):
    B, S, D = x.shape
    depth = w_in.shape[0]
    M = B * S
    alpha = (2 * depth) ** 0.25
    R = D // 2
    H = R // RET_HEAD_DIM
    Hy = D // 4
    Lw = D // 4
    col_hy = 4 * R
    col_lg = col_hy + 3 * Hy
    col_lx = col_lg + Lw

    pos = jnp.arange(S, dtype=F32)
    half = RET_HEAD_DIM // 2
    inv_freq = 1.0 / (10000.0 ** jnp.linspace(0.0, 1.0, half, dtype=F32))
    ang = pos[:, None] * inv_freq[None, :]
    cos, sin = jnp.cos(ang), jnp.sin(ang)
    log_g = jnp.log1p(-jnp.exp2(-5.0 - jnp.arange(H, dtype=F32)))
    lg_tab = jnp.broadcast_to(log_g[:, None, None], (H, 8, max(512, RET_HEAD_DIM)))
    tabs = _dft_tables(S)
    m = jnp.arange(2 * S)
    fpos = jnp.where(m < S, m, 2 * S - m).astype(F32)
    emb = hy_filt_w1.shape[1]
    bands = (emb - 1) // 2
    fr = jnp.linspace(1e-4, bands - 1, bands, dtype=F32)
    wang = 2.0 * math.pi * fpos / S
    zfeat = jnp.concatenate([(fpos / max(S - 1, 1))[:, None], jnp.cos(wang[:, None] * fr),
                             -jnp.sin(wang[:, None] * fr)], axis=-1)
    epad = LANES
    zfeat = jnp.pad(zfeat, ((0, 0), (0, epad - emb)))

    assert B == 2, "the Hyena DFT carries the two batch rows as one complex signal"
    ffa = (_pack_gate_up(ffa_w_gate, ffa_w_up), _to_bf16(ffa_w_down))
    ffb = (_pack_gate_up(ffb_w_gate, ffb_w_up), _to_bf16(ffb_w_down))
    w_in_b, w_out_b = _to_bf16(w_in), _to_bf16(w_out)
    row = lambda v: v.reshape(1, -1)

    h = x.reshape(M, D)
    for l in range(depth):
        h = _ffn(h, *ffa, l, row(ln_gain[l, 0]), row(ln_bias[l, 0]), alpha)
        p3 = _inproj(h, w_in_b, l).reshape(B, S, -1)
        y_ret = _retention(p3, cos, sin, lg_tab, H)
        w1p = jnp.pad(hy_filt_w1[l], ((0, epad - emb), (0, 0)))
        filt_params = (w1p, row(hy_filt_b1[l]), row(hy_filt_freq1[l]), hy_filt_w2[l],
                       row(hy_filt_b2[l]), row(hy_filt_freq2[l]), hy_filt_w3[l], row(hy_decay[l]))
        y_hy = _hyena(p3, col_hy, Hy, tabs, hy_conv_w[l], hy_conv_b[l], filt_params, hy_bias[l], zfeat)
        y_lru = _lru(p3, col_lg, col_lx, Lw, lru_conv_w[l], lru_conv_b[l], lru_wa[l], lru_ba[l],
                     lru_wx[l], lru_bx[l], lru_lambda[l])
        yb = _mixnorm(y_ret.reshape(M, R), y_hy, y_lru.reshape(M, Lw),
                      row(mix_norm_gain[l]))
        h = _outproj(yb, w_out_b, l, h, row(ln_gain[l, 1]), row(ln_bias[l, 1]), alpha)
        h = _ffn(h, *ffb, l, row(ln_gain[l, 2]), row(ln_bias[l, 2]), alpha)
    return h.reshape(B, S, D)
```

```python
import functools
import math

import numpy as np
import jax
import jax.numpy as jnp
from jax import lax
from jax.experimental import pallas as pl
from jax.experimental.pallas import tpu as pltpu

F32 = jnp.float32
BF16 = jnp.bfloat16

LANES = 128
RET_HEAD_DIM = 256
LRU_BLOCK_DIM = 128
LRU_C = 8.0
VMEM_CAP = 60 * 1024 * 1024
LN_ROWS = 16


def _params(semantics, vmem_bytes):
    return pltpu.CompilerParams(
        dimension_semantics=semantics,
        vmem_limit_bytes=int(min(VMEM_CAP, max(vmem_bytes, 16 * 1024 * 1024))))


def _layer_norm_rows(y, g, b, eps=1e-5):
    mu = jnp.mean(y, axis=-1, keepdims=True)
    d = y - mu
    var = jnp.mean(d * d, axis=-1, keepdims=True)
    return d * lax.rsqrt(var + eps) * g + b


def _silu(x):
    return x * jax.nn.sigmoid(x)


def _ffn_body(x_ref, wgu_a, wd_a, g_ref, b_ref, o_ref, xb_ref, *, alpha, rows, slab):
    j = pl.program_id(1)
    nj = pl.num_programs(1)
    tm = x_ref.shape[0]
    tf = wd_a.shape[0]
    D = o_ref.shape[1]

    @pl.when(j == 0)
    def _():
        xb_ref[...] = x_ref[...].astype(BF16)
        o_ref[...] = jnp.zeros_like(o_ref)

    def hidden(wgu_ref):
        gu = jnp.dot(xb_ref[...], wgu_ref[...], preferred_element_type=F32)
        return (_silu(gu[:, :tf]) * gu[:, tf:]).astype(BF16)

    def accumulate(pairs):
        hs = [(hidden(wgu_ref), wd_ref) for wgu_ref, wd_ref in pairs]
        for c0 in range(0, D, slab):
            part = None
            for h, wd_ref in hs:
                p = jnp.dot(h, wd_ref[:, c0:c0 + slab], preferred_element_type=F32)
                part = p if part is None else part + p
            o_ref[:, c0:c0 + slab] += part

    accumulate([(wgu_a, wd_a)])

    @pl.when(j == nj - 1)
    def _():
        g = g_ref[...]
        b = b_ref[...]

        def body(r, carry):
            sl = pl.ds(pl.multiple_of(r * rows, rows), rows)
            y = alpha * x_ref[sl, :] + 0.5 * o_ref[sl, :]
            o_ref[sl, :] = _layer_norm_rows(y, g, b)
            return carry

        lax.fori_loop(0, tm // rows, body, 0, unroll=2)


def _cast_body(x_ref, o_ref):
    o_ref[...] = x_ref[...].astype(BF16)


def _to_bf16(w):
    L, R, C = w.shape
    tr = 256
    while tr > 8 and (R % tr or tr * C * 4 > (12 << 20)):
        tr //= 2
    spec = pl.BlockSpec((None, tr, C), lambda l, i: (l, i, 0))
    return pl.pallas_call(
        _cast_body,
        grid=(L, R // tr),
        in_specs=[spec],
        out_specs=spec,
        out_shape=jax.ShapeDtypeStruct(w.shape, BF16),
        compiler_params=_params(("parallel", "parallel"), 3 * tr * C * 4 + (8 << 20)),
        name="to_bf16",
    )(w)


FFN_TF = 256


def _pack_gate_up_body(g_ref, u_ref, o_ref, *, tf):
    for c in range(g_ref.shape[1] // tf):
        o_ref[c, :, :tf] = g_ref[:, c * tf:(c + 1) * tf].astype(BF16)
        o_ref[c, :, tf:] = u_ref[:, c * tf:(c + 1) * tf].astype(BF16)


def _pack_gate_up(wg, wu):
    L, D, F = wg.shape
    tf = FFN_TF if F % FFN_TF == 0 else F
    tr = min(128, D)
    spec = pl.BlockSpec((None, tr, F), lambda l, i: (l, i, 0))
    return pl.pallas_call(
        functools.partial(_pack_gate_up_body, tf=tf),
        grid=(L, D // tr),
        in_specs=[spec, spec],
        out_specs=pl.BlockSpec((None, F // tf, tr, 2 * tf), lambda l, i: (l, 0, i, 0)),
        out_shape=jax.ShapeDtypeStruct((L, F // tf, D, 2 * tf), BF16),
        compiler_params=_params(("parallel", "parallel"), 6 * tr * F * 4 + (8 << 20)),
        name="pack_gate_up",
    )(wg, wu)


def _ffn(x, wgu, wd, layer, g, b, alpha):
    M, D = x.shape
    F = wd.shape[1]
    tm = min(1024, M)
    tf = FFN_TF if F % FFN_TF == 0 else F
    nf = F // tf
    rows = min(LN_ROWS, tm)
    slab = 512 if D % 512 == 0 else D
    once = pl.Buffered(1)
    vmem = (tm * D * 4 + tm * D * 2 + tm * D * 4 + 2 * 3 * D * tf * 2 + 6 * tm * tf * 4
            + 2 * tm * slab * 4)
    return pl.pallas_call(
        functools.partial(_ffn_body, alpha=alpha, rows=rows, slab=slab),
        grid=(M // tm, nf),
        in_specs=[
            pl.BlockSpec((tm, D), lambda i, j: (i, 0), pipeline_mode=once),
            pl.BlockSpec((None, None, D, 2 * tf), lambda i, j: (layer, j, 0, 0)),
            pl.BlockSpec((None, tf, D), lambda i, j: (layer, j, 0)),
            pl.BlockSpec((1, D), lambda i, j: (0, 0)),
            pl.BlockSpec((1, D), lambda i, j: (0, 0)),
        ],
        out_specs=pl.BlockSpec((tm, D), lambda i, j: (i, 0), pipeline_mode=once),
        out_shape=jax.ShapeDtypeStruct((M, D), F32),
        scratch_shapes=[pltpu.VMEM((tm, D), BF16)],
        compiler_params=_params(("parallel", "arbitrary"), vmem + (4 << 20)),
        name="ffn",
    )(x, wgu, wd, g, b)


def _inproj_body(x_ref, w_ref, o_ref, xb_ref):
    @pl.when(pl.program_id(1) == 0)
    def _():
        xb_ref[...] = x_ref[...].astype(BF16)

    o_ref[...] = jnp.dot(xb_ref[...], w_ref[...], preferred_element_type=F32)


def _inproj(x, w, layer):
    M, K = x.shape
    N = w.shape[2]
    tm = min(1024, M)
    tn = 1024 if N % 1024 == 0 else (512 if N % 512 == 0 else 128)
    vmem = tm * K * 4 + tm * K * 2 + 2 * K * tn * 2 + 3 * tm * tn * 4
    return pl.pallas_call(
        _inproj_body,
        grid=(M // tm, N // tn),
        in_specs=[
            pl.BlockSpec((tm, K), lambda i, j: (i, 0), pipeline_mode=pl.Buffered(1)),
            pl.BlockSpec((None, K, tn), lambda i, j: (layer, 0, j)),
        ],
        out_specs=pl.BlockSpec((tm, tn), lambda i, j: (i, j)),
        out_shape=jax.ShapeDtypeStruct((M, N), F32),
        scratch_shapes=[pltpu.VMEM((tm, K), BF16)],
        compiler_params=_params(("parallel", "arbitrary"), vmem + (4 << 20)),
        name="inproj",
    )(x, w)


def _outproj_body(y_ref, w_ref, x_ref, g_ref, b_ref, o_ref, *, alpha, rows, nj, slab):
    j = pl.program_id(1)
    tm, D = o_ref.shape
    rc = x_ref.shape[0]

    @pl.when(j == 0)
    def _():
        o_ref[...] = jnp.zeros_like(o_ref)

    yk = y_ref[...]
    for c0 in range(0, D, slab):
        o_ref[:, c0:c0 + slab] += jnp.dot(yk, w_ref[:, c0:c0 + slab], preferred_element_type=F32)
    sl = pl.ds(pl.multiple_of(j * rc, rc), rc)
    o_ref[sl, :] += alpha * x_ref[...]

    @pl.when(j == nj - 1)
    def _():
        g = g_ref[...]
        b = b_ref[...]

        def body(r, carry):
            rs = pl.ds(pl.multiple_of(r * rows, rows), rows)
            o_ref[rs, :] = _layer_norm_rows(o_ref[rs, :], g, b)
            return carry

        lax.fori_loop(0, tm // rows, body, 0, unroll=2)


def _outproj(y, w, layer, x, g, b, alpha):
    M, K = y.shape
    D = w.shape[2]
    tm = min(512, M)
    tk = 1024 if K % 1024 == 0 else K
    nj = K // tk
    rc = tm // nj
    rows = min(LN_ROWS, tm)
    slab = 1024 if D % 1024 == 0 else D
    vmem = 2 * tm * tk * 2 + 2 * tk * D * 2 + 2 * rc * D * 4 + 2 * tm * D * 4 + 2 * tm * slab * 4
    return pl.pallas_call(
        functools.partial(_outproj_body, alpha=alpha, rows=rows, nj=nj, slab=slab),
        grid=(M // tm, nj),
        in_specs=[
            pl.BlockSpec((tm, tk), lambda i, j: (i, j)),
            pl.BlockSpec((None, tk, D), lambda i, j: (layer, j, 0)),
            pl.BlockSpec((rc, D), lambda i, j: (i * nj + j, 0)),
            pl.BlockSpec((1, D), lambda i, j: (0, 0)),
            pl.BlockSpec((1, D), lambda i, j: (0, 0)),
        ],
        out_specs=pl.BlockSpec((tm, D), lambda i, j: (i, 0)),
        out_shape=jax.ShapeDtypeStruct((M, D), F32),
        compiler_params=_params(("parallel", "arbitrary"), vmem + (6 << 20)),
        name="outproj",
    )(y, w, x, g, b)


def _rotary(x, cos, sin):
    half = x.shape[-1] // 2
    x1 = x[:, :half]
    x2 = x[:, half:]
    return jnp.concatenate([x1 * cos - x2 * sin, x1 * sin + x2 * cos], axis=-1)


def _ret_qkv(q_ref, k_ref, cos_ref, sin_ref, chunk):
    C = q_ref.shape[0]
    rows = pl.ds(pl.multiple_of(chunk * C, C), C)
    cos = cos_ref[rows, :]
    sin = sin_ref[rows, :]
    q = _rotary(q_ref[...], cos, sin)
    k = _rotary(k_ref[...], cos, sin) * (RET_HEAD_DIM ** -0.5)
    return q, k


def _row_index(shape):
    return lax.broadcasted_iota(jnp.int32, shape, 0).astype(F32)


def _ret_fwd_body(q_ref, k_ref, v_ref, cos_ref, sin_ref, lg_ref, y_ref,
                  state_ref, qd_ref, kd_ref, intra_ref):
    C, dk = q_ref.shape
    lg = lg_ref[0:1, :dk]

    @pl.when(pl.program_id(2) == 0)
    def _():
        state_ref[...] = jnp.zeros_like(state_ref)
        row = _row_index((C, dk))
        qd_ref[...] = jnp.exp(lg * (row + 1.0))
        kd_ref[...] = jnp.exp(lg * (C - 1.0 - row))
        ii = lax.broadcasted_iota(jnp.int32, (C, C), 0)
        jj = lax.broadcasted_iota(jnp.int32, (C, C), 1)
        intra_ref[...] = jnp.exp(lg_ref[0:1, :C] * jnp.abs(ii - jj).astype(F32))

    q, k = _ret_qkv(q_ref, k_ref, cos_ref, sin_ref, pl.program_id(2))
    vb = v_ref[...].astype(BF16)
    q_in = (q * qd_ref[...]).astype(BF16)
    k_out = (k * kd_ref[...]).astype(BF16)
    s = lax.dot_general(q.astype(BF16), k.astype(BF16), (((1,), (1,)), ((), ())),
                        preferred_element_type=F32) * intra_ref[...]
    state = state_ref[...]
    y = jnp.dot(s.astype(BF16), vb, preferred_element_type=F32)
    y = y + jnp.dot(q_in, state.astype(BF16), preferred_element_type=F32)
    y_ref[...] = y
    kv = lax.dot_general(k_out, vb, (((0,), (0,)), ((), ())), preferred_element_type=F32)
    state_ref[...] = state * jnp.exp(lg * float(C)) + kv


def _ret_bwd_body(q_ref, k_ref, v_ref, g_ref, cos_ref, sin_ref, lg_ref, yf_ref, y_ref,
                  state_ref, qd_ref, kd_ref):
    C, dk = q_ref.shape
    lg = lg_ref[0:1, :dk]

    @pl.when(pl.program_id(2) == 0)
    def _():
        state_ref[...] = jnp.zeros_like(state_ref)
        row = _row_index((C, dk))
        qd_ref[...] = jnp.exp(lg * (C - row))
        kd_ref[...] = jnp.exp(lg * row)

    q, k = _ret_qkv(q_ref, k_ref, cos_ref, sin_ref, pl.num_programs(2) - 1 - pl.program_id(2))
    vb = v_ref[...].astype(BF16)
    q_in = (q * qd_ref[...]).astype(BF16)
    k_out = (k * kd_ref[...]).astype(BF16)
    state = state_ref[...]
    y = yf_ref[...] + jnp.dot(q_in, state.astype(BF16), preferred_element_type=F32)
    kv = lax.dot_general(k_out, vb, (((0,), (0,)), ((), ())), preferred_element_type=F32)
    state_ref[...] = state * jnp.exp(lg * float(C)) + kv
    mu = jnp.mean(y, axis=-1, keepdims=True)
    d = y - mu
    var = jnp.mean(d * d, axis=-1, keepdims=True)
    y_ref[...] = _silu(g_ref[...]) * (d * lax.rsqrt(var + 1e-5))


def _retention(p3, cos, sin, lg_tab, H):
    B, S, _ = p3.shape
    dk = RET_HEAD_DIM
    C = min(512, S)
    nc = S // C
    W = lg_tab.shape[-1]
    blk = lambda off, rev: pl.BlockSpec(
        (None, C, dk),
        (lambda b, h, c: (b, nc - 1 - c, off + h)) if rev else (lambda b, h, c: (b, c, off + h)))
    Cb = C
    ncb = nc
    blk_b = lambda off: pl.BlockSpec((None, Cb, dk), lambda b, h, c: (b, ncb - 1 - c, off + h))
    tab = lambda rev: pl.BlockSpec((S, dk // 2), lambda b, h, c: (0, 0))
    lg_spec = pl.BlockSpec((None, 8, W), lambda b, h, c: (h, 0, 0))
    vmem = 32 << 20
    y_f = pl.pallas_call(
        _ret_fwd_body,
        grid=(B, H, nc),
        in_specs=[blk(0, False), blk(H, False), blk(2 * H, False), tab(False), tab(False), lg_spec],
        out_specs=pl.BlockSpec((None, C, dk), lambda b, h, c: (b, c, h)),
        out_shape=jax.ShapeDtypeStruct((B, S, H * dk), F32),
        scratch_shapes=[pltpu.VMEM((dk, dk), F32), pltpu.VMEM((C, dk), F32), pltpu.VMEM((C, dk), F32),
                        pltpu.VMEM((C, C), F32)],
        compiler_params=_params(("parallel", "parallel", "arbitrary"), vmem),
        name="ret_fwd",
    )(p3, p3, p3, cos, sin, lg_tab)
    return pl.pallas_call(
        _ret_bwd_body,
        grid=(B, H, ncb),
        in_specs=[blk_b(0), blk_b(H), blk_b(2 * H), blk_b(3 * H),
                  tab(True), tab(True), lg_spec, blk_b(0)],
        out_specs=blk_b(0),
        out_shape=jax.ShapeDtypeStruct((B, S, H * dk), F32),
        scratch_shapes=[pltpu.VMEM((dk, dk), F32), pltpu.VMEM((Cb, dk), F32), pltpu.VMEM((Cb, dk), F32)],
        compiler_params=_params(("parallel", "parallel", "arbitrary"), vmem),
        name="ret_bwd",
    )(p3, p3, p3, p3, cos, sin, lg_tab, y_f)


HALO = 8


def _fill_haloed(xs_ref, x_ref, S):
    zeros = jnp.zeros((HALO, xs_ref.shape[1]), F32)
    xs_ref[0:HALO, :] = zeros
    xs_ref[S + HALO:S + 2 * HALO, :] = zeros
    xs_ref[HALO:S + HALO, :] = x_ref[...]


def _conv_rows(xs_ref, r0, R, taps, left, bias):
    ext = xs_ref[pl.ds(r0, R + 2 * HALO), :]
    n = R + 2 * HALO
    acc = None
    for j, w in enumerate(taps):
        d = j - left
        sh = ext if d == 0 else pltpu.roll(ext, (-d) % n, axis=0)
        term = w * sh[HALO:HALO + R, :]
        acc = term if acc is None else acc + term
    return acc + bias


FFT_N2 = 128
TN2 = 32
NT = FFT_N2 // TN2
PITCH = TN2 + 4
STAGE_CB = LANES


def _store_tiled_pair(o_ref, lead, blocks):
    pad = jnp.zeros((PITCH - TN2, blocks[0].shape[1]), F32)
    for h in range(NT):
        for q, block in enumerate(blocks):
            o_ref[(h, *lead, slice(q * PITCH, q * PITCH + TN2))] = block[h * TN2:(h + 1) * TN2, :]
            o_ref[(h, *lead, slice(q * PITCH + TN2, (q + 1) * PITCH))] = pad


def _hy_conv_body(u_ref, w_ref, b_ref, o_ref, xs_ref, *, S, R, K):
    _fill_haloed(xs_ref, u_ref, S)
    taps = [w_ref[j:j + 1, :] for j in range(K)]
    bias = b_ref[...]

    def body(i, carry):
        r0 = pl.multiple_of(i * R, R)
        res = _conv_rows(xs_ref, r0, R, taps, (K - 1) // 2, bias)
        _store_tiled_pair(o_ref, (i,), [res[:FFT_N2], res[FFT_N2:]])
        return carry

    lax.fori_loop(0, S // R, body, 0)


def _hy_conv(p3, col0, width, w, b):
    B, S, _ = p3.shape
    K = w.shape[0]
    cb = LANES
    nb = width // cb
    C = width // 3
    R = 2 * FFT_N2
    npair = S // R
    wp = jnp.zeros((8, width), F32).at[:K].set(w)
    off = col0 // cb
    per = C // cb
    return pl.pallas_call(
        functools.partial(_hy_conv_body, S=S, R=R, K=K),
        grid=(B, nb),
        in_specs=[
            pl.BlockSpec((None, S, cb), lambda b_, j: (b_, 0, off + j)),
            pl.BlockSpec((8, cb), lambda b_, j: (0, j)),
            pl.BlockSpec((1, cb), lambda b_, j: (0, j)),
        ],
        out_specs=pl.BlockSpec((None, NT, npair, 2 * PITCH, cb),
                               lambda b_, j: ((j // per) * B + b_, 0, 0, 0, j % per)),
        out_shape=jax.ShapeDtypeStruct((3 * B, NT, npair, 2 * PITCH, C), F32),
        scratch_shapes=[pltpu.VMEM((S + 2 * HALO, cb), F32)],
        compiler_params=_params(("parallel", "parallel"), 7 * S * cb * 4 + (8 << 20)),
        name="hy_conv",
    )(p3, wp, b.reshape(1, width)).reshape(3 * B, NT, npair * 2 * PITCH, C)


def _dot_split(a, b):
    a_hi = a.astype(BF16).astype(F32)
    b_hi = b.astype(BF16).astype(F32)
    lhs = jnp.concatenate([a_hi, a_hi, a - a_hi], axis=1).astype(BF16)
    rhs = jnp.concatenate([b_hi, b - b_hi, b_hi], axis=0).astype(BF16)
    return jnp.dot(lhs, rhs, preferred_element_type=F32)


def _hy_filter_body(z_ref, w1_ref, b1_ref, f1_ref, w2_ref, b2_ref, f2_ref, w3_ref, dec_ref,
                    o_ref, sum_ref, *, L, tr):
    i = pl.program_id(0)
    z = z_ref[...]
    h = jnp.sin(f1_ref[...] * (_dot_split(z, w1_ref[...]) + b1_ref[...]))
    h = jnp.sin(f2_ref[...] * (_dot_split(h, w2_ref[...]) + b2_ref[...]))
    h = _dot_split(h, w3_ref[...])
    t = z[:, 0:1]
    h = h * jnp.exp(-t * jnp.abs(dec_ref[...]))
    m = i * tr + lax.broadcasted_iota(jnp.int32, h.shape, 0)
    filt = jnp.where(m == L, 0.0, h)
    _store_tiled_pair(o_ref, (), [filt[:FFT_N2], filt[FFT_N2:]])

    @pl.when(i == 0)
    def _():
        sum_ref[...] = jnp.zeros_like(sum_ref)

    sum_ref[...] += jnp.sum(jnp.abs(filt), axis=0, keepdims=True)


def _hy_filter(zfeat, w1, b1, f1, w2, b2, f2, w3, dec, L):
    N, E = zfeat.shape
    hid = w1.shape[1]
    C4 = w3.shape[1]
    tr = 2 * FFT_N2
    nt_half = L // tr
    full = lambda a: pl.BlockSpec(a.shape, lambda i: (0,) * a.ndim)
    side = lambda a: pl.BlockSpec((a.shape[0], C4 // 2), lambda i: (0, i // nt_half))
    args = (w1, b1, f1, w2, b2, f2, w3, dec)
    return pl.pallas_call(
        functools.partial(_hy_filter_body, L=L, tr=tr),
        grid=(N // tr,),
        in_specs=[pl.BlockSpec((tr, E), lambda i: (i, 0))] + [full(a) for a in args[:6]]
        + [side(w3), side(dec)],
        out_specs=[pl.BlockSpec((NT, None, 2 * PITCH, C4 // 2), lambda i: (0, i, 0, 0)),
                   pl.BlockSpec((1, C4 // 2), lambda i: (0, 0))],
        out_shape=[jax.ShapeDtypeStruct((NT, N // tr, 2 * PITCH, C4 // 2), F32),
                   jax.ShapeDtypeStruct((1, C4 // 2), F32)],
        compiler_params=_params(("arbitrary",), 32 << 20),
        name="hy_filter",
    )(zfeat, *args)


def _stage_a_body(w_ref, x_ref, o_ref, *, nparts, rows):
    mo = w_ref.shape[0]
    w = w_ref[...]

    def body(r, carry):
        parts = [x_ref[p, pl.ds(r, rows, stride=PITCH), :] for p in range(nparts)]
        x = parts[0] if nparts == 1 else jnp.concatenate(parts, axis=0)
        o_ref[pl.ds(r, mo, stride=PITCH), :] = jnp.dot(w, x.astype(BF16), preferred_element_type=F32)
        return carry

    lax.fori_loop(0, TN2, body, 0, unroll=8)
    pad = jnp.zeros((mo, o_ref.shape[1]), F32)
    for r in range(TN2, PITCH):
        o_ref[pl.ds(r, mo, stride=PITCH), :] = pad


def _stage_a(wmat, x, group, nparts):
    mo, K = wmat.shape
    _, _, T, C = x.shape
    rows = T // PITCH
    assert K == nparts * rows
    cb = min(STAGE_CB, C)
    vmem = 2 * nparts * T * cb * 4 + 2 * mo * PITCH * cb * 4 + (12 << 20)
    return pl.pallas_call(
        functools.partial(_stage_a_body, nparts=nparts, rows=rows),
        grid=(C // cb, NT),
        in_specs=[pl.BlockSpec((mo, K), lambda j, h: (0, 0)),
                  pl.BlockSpec((nparts, None, T, cb), lambda j, h: (group, h, 0, j))],
        out_specs=pl.BlockSpec((None, mo * PITCH, cb), lambda j, h: (h, 0, j)),
        out_shape=jax.ShapeDtypeStruct((NT, mo * PITCH, C), F32),
        compiler_params=_params(("parallel", "parallel"), vmem),
        name="hy_stage_a",
    )(wmat, x)


def _hy_freq_body(g_ref, gi_ref, x_ref, f_ref, s_ref, o_ref, *, kb):
    n2 = FFT_N2
    inv = 1.0 / (s_ref[...] + 1e-6)

    def gather(ref, t):
        lo = (t % 2) * PITCH
        return jnp.concatenate([ref[h, ri, t // 2, lo:lo + TN2, :] for ri in range(2) for h in range(NT)],
                               axis=0)

    pad = jnp.zeros((PITCH - TN2, o_ref.shape[-1]), F32)
    for t in range(kb):
        g = g_ref[t]
        y = jnp.dot(g, gather(x_ref, t).astype(BF16), preferred_element_type=F32)
        h = jnp.dot(g, gather(f_ref, t).astype(BF16), preferred_element_type=F32) * inv
        yr, yi = y[:n2], y[n2:]
        hr, hi = h[:n2], h[n2:]
        z = jnp.concatenate([yr * hr - yi * hi, yr * hi + yi * hr], axis=0).astype(BF16)
        e = jnp.dot(gi_ref[t], z, preferred_element_type=F32)
        lo = (t % 2) * PITCH
        for ri in range(2):
            for hh in range(NT):
                o_ref[hh, ri, t // 2, lo:lo + TN2, :] = e[ri * n2 + hh * TN2:ri * n2 + (hh + 1) * TN2]
                o_ref[hh, ri, t // 2, lo + TN2:lo + PITCH, :] = pad


def _hy_freq(gmat, gimat, xa, fa, asum, order, C):
    N1 = 2 * xa.shape[2]
    n2 = FFT_N2
    kb = min(4, N1)
    blk = pl.BlockSpec((NT, 2, kb // 2, 2 * PITCH, C), lambda i: (0, 0, i, 0, 0))
    gspec = pl.BlockSpec((kb, 2 * n2, 2 * n2), lambda i: (i, 0, 0))
    return pl.pallas_call(
        functools.partial(_hy_freq_body, kb=kb),
        grid=(N1 // kb,),
        in_specs=[gspec, gspec, blk,
                  pl.BlockSpec((NT, 2, kb // 2, 2 * PITCH, C), lambda i: (0, 0, i, 0, order)),
                  pl.BlockSpec((1, C), lambda i: (0, order))],
        out_specs=blk,
        out_shape=jax.ShapeDtypeStruct(xa.shape, F32),
        compiler_params=_params(("parallel",), 6 * 2 * kb * n2 * C * 4 + 24 * n2 * C * 4 + (8 << 20)),
        name="hy_freq",
    )(gmat, gimat, xa, fa, asum)


def _stage_d_body(w_ref, e_ref, z_ref, x_ref, b_ref, o_ref, *, B, nh):
    w = w_ref[...]
    bias = b_ref[...]
    mi = w.shape[1]

    def body(r, carry):
        e = e_ref[pl.ds(r, mi, stride=PITCH), :].astype(BF16)
        conv = jnp.dot(w, e, preferred_element_type=F32)
        rows = pl.ds(r, nh, stride=PITCH)
        for b in range(B):
            o_ref[b, rows, :] = x_ref[b, rows, :] * (conv[b * nh:(b + 1) * nh] + bias * z_ref[b, rows, :])
        return carry

    lax.fori_loop(0, TN2, body, 0, unroll=8)
    pad = jnp.zeros((nh, o_ref.shape[-1]), F32)
    for r in range(TN2, PITCH):
        for b in range(B):
            o_ref[b, pl.ds(r, nh, stride=PITCH), :] = pad


def _stage_d(w4, e, z, z_group, gate, gate_group, bias, B):
    mo, mi = w4.shape
    nh = mo // B
    T, C = z.shape[2], z.shape[3]
    cb = min(STAGE_CB, C)
    nat = lambda grp: pl.BlockSpec((B, None, T, cb), lambda j, h: (grp, h, 0, j))
    vmem = 2 * mi * PITCH * cb * 4 + 6 * B * T * cb * 4 + (12 << 20)
    return pl.pallas_call(
        functools.partial(_stage_d_body, B=B, nh=nh),
        grid=(C // cb, NT),
        in_specs=[pl.BlockSpec((mo, mi), lambda j, h: (0, 0)),
                  pl.BlockSpec((None, mi * PITCH, cb), lambda j, h: (h, 0, j)),
                  nat(z_group), nat(gate_group),
                  pl.BlockSpec((1, cb), lambda j, h: (0, j))],
        out_specs=nat(0),
        out_shape=jax.ShapeDtypeStruct((B, NT, T, C), F32),
        compiler_params=_params(("parallel", "parallel"), vmem),
        name="hy_stage_d",
    )(w4, e, z, gate, bias)


def _dft_tables(S):
    N = 2 * S
    n2 = FFT_N2
    N1 = N // n2
    nh = S // n2
    k1 = np.arange(N1)
    f1 = np.exp(-2j * np.pi * np.outer(k1, np.arange(N1)) / N1)
    fa = f1[:, :nh]
    wa = np.block([[fa.real, -fa.imag], [fa.imag, fa.real]])
    wa_real = np.concatenate([f1.real, f1.imag], axis=0)
    a2 = np.arange(n2)
    base = np.exp(-2j * np.pi * np.outer(a2, a2) / n2)
    tw = np.exp(-2j * np.pi * np.outer(k1, a2) / N)
    g = base[None, :, :] * tw[:, None, :]
    gi = np.conj(np.transpose(g, (0, 2, 1)))
    blockify = lambda c: np.concatenate(
        [np.concatenate([c.real, -c.imag], axis=-1), np.concatenate([c.imag, c.real], axis=-1)], axis=-2)
    f4 = np.exp(2j * np.pi * np.outer(np.arange(nh), k1) / N1) / N
    wd = np.block([[f4.real, -f4.imag], [f4.imag, f4.real]])
    cast = lambda a: jnp.asarray(a, dtype=F32).astype(BF16)
    return dict(N1=N1, nh=nh, wa=cast(wa), wa_real=cast(wa_real), g=cast(blockify(g)),
                gi=cast(blockify(gi)), wd=cast(wd))


def _hyena(p3, col0, C, tabs, conv_w, conv_b, filt_params, hy_bias, zfeat):
    B, S, _ = p3.shape
    N1 = tabs["N1"]
    u = _hy_conv(p3, col0, 3 * C, conv_w, conv_b)
    filt, asum = _hy_filter(zfeat, *filt_params, S)
    filt = filt.reshape(1, NT, N1 * PITCH, 2 * C)
    fa = _stage_a(tabs["wa_real"], filt, 0, 1).reshape(NT, 2, N1 // 2, 2 * PITCH, 2 * C)
    z = u
    for order in range(2):
        xa = _stage_a(tabs["wa"], z, 0, B)
        e = _hy_freq(tabs["g"], tabs["gi"], xa.reshape(NT, 2, N1 // 2, 2 * PITCH, C), fa, asum, order, C)
        z = _stage_d(tabs["wd"], e.reshape(NT, 2 * N1 * PITCH, C), z, 0, u, 1 + order,
                     hy_bias[order].reshape(1, C), B)
    return z


def _scan_rows(a, b, reverse):
    R = a.shape[0]
    row = lax.broadcasted_iota(jnp.int32, a.shape, 0)
    d = 1
    while d < R:
        if reverse:
            a_s = pltpu.roll(a, R - d, axis=0)
            b_s = pltpu.roll(b, R - d, axis=0)
            m = row < R - d
        else:
            a_s = pltpu.roll(a, d, axis=0)
            b_s = pltpu.roll(b, d, axis=0)
            m = row >= d
        b = jnp.where(m, a * b_s + b, b)
        a = jnp.where(m, a * a_s, a)
        d *= 2
    return a, b


def _gelu_tanh(x):
    return 0.5 * x * (1.0 + jnp.tanh(0.7978845608028654 * (x + 0.044715 * x * x * x)))


def _lru_body(g_ref, x_ref, cw_ref, cb_ref, w_ref, bias_ref, lam_ref, o_ref,
              xs_ref, hf_ref, ab_ref, bb_ref, *, S, R, K):
    bd = x_ref.shape[1]
    _fill_haloed(xs_ref, x_ref, S)
    taps = [cw_ref[j:j + 1, :] for j in range(K)]
    cbias = cb_ref[...]
    w = w_ref[...]
    bias = bias_ref[...]
    nl = -lam_ref[...]
    sp = jnp.maximum(nl, 0.0) + jnp.log1p(jnp.exp(-jnp.abs(nl)))
    sp0 = sp[0:1, :]
    sp1 = sp[1:2, :]
    nchunk = S // R

    def gates(r, i, spd, xr):
        log_a = -LRU_C * jax.nn.sigmoid(r) * spd
        a = jnp.exp(log_a)
        b = jnp.sqrt(1.0 - a * a) * (jax.nn.sigmoid(i) * xr)
        return a, b

    def fwd(i, hc):
        r0 = pl.multiple_of(i * R, R)
        xr = _conv_rows(xs_ref, r0, R, taps, (K - 1) // 2, cbias)
        proj = jnp.dot(xr.astype(BF16), w, preferred_element_type=F32) + bias
        a0, b0 = gates(proj[:, 0:bd], proj[:, bd:2 * bd], sp0, xr)
        a1, b1 = gates(proj[:, 2 * bd:3 * bd], proj[:, 3 * bd:4 * bd], sp1, xr)
        ab_ref[pl.ds(r0, R), :] = a1
        bb_ref[pl.ds(r0, R), :] = b1
        ac, hl = _scan_rows(a0, b0, reverse=False)
        h = ac * hc + hl
        hf_ref[pl.ds(r0, R), :] = h
        return h[R - 1:R, :]

    lax.fori_loop(0, nchunk, fwd, jnp.zeros((1, bd), F32))

    def bwd(t, hc):
        r0 = pl.multiple_of((nchunk - 1 - t) * R, R)
        sl = pl.ds(r0, R)
        ac, hl = _scan_rows(ab_ref[sl, :], bb_ref[sl, :], reverse=True)
        h = ac * hc + hl
        o_ref[sl, :] = _gelu_tanh(g_ref[sl, :]) * (hf_ref[sl, :] + h)
        return h[0:1, :]

    lax.fori_loop(0, nchunk, bwd, jnp.zeros((1, bd), F32))


def _lru(p3, col_g, col_x, width, conv_w, conv_b, wa, ba, wx, bx, lam):
    B, S, _ = p3.shape
    bd = LRU_BLOCK_DIM
    nb = width // bd
    K = conv_w.shape[0]
    R = min(256, S)
    cw = jnp.zeros((8, width), F32).at[:K].set(conv_w)
    wcat = jnp.concatenate([wa[0], wx[0], wa[1], wx[1]], axis=-1).astype(BF16)
    bsplit = lambda v: v.reshape(nb, 1, bd)
    bcat = jnp.concatenate([bsplit(ba[0]), bsplit(bx[0]), bsplit(ba[1]), bsplit(bx[1])], axis=-1)
    lam8 = jnp.zeros((8, width), F32).at[:2].set(lam)
    og, ox = col_g // bd, col_x // bd
    return pl.pallas_call(
        functools.partial(_lru_body, S=S, R=R, K=K),
        grid=(B, nb),
        in_specs=[
            pl.BlockSpec((None, S, bd), lambda b, j: (b, 0, og + j)),
            pl.BlockSpec((None, S, bd), lambda b, j: (b, 0, ox + j)),
            pl.BlockSpec((8, bd), lambda b, j: (0, j)),
            pl.BlockSpec((1, bd), lambda b, j: (0, j)),
            pl.BlockSpec((None, bd, 4 * bd), lambda b, j: (j, 0, 0)),
            pl.BlockSpec((None, 1, 4 * bd), lambda b, j: (j, 0, 0)),
            pl.BlockSpec((8, bd), lambda b, j: (0, j)),
        ],
        out_specs=pl.BlockSpec((None, S, bd), lambda b, j: (b, 0, j)),
        out_shape=jax.ShapeDtypeStruct((B, S, width), F32),
        scratch_shapes=[pltpu.VMEM((S + 2 * HALO, bd), F32), pltpu.VMEM((S, bd), F32),
                        pltpu.VMEM((S, bd), F32), pltpu.VMEM((S, bd), F32)],
        compiler_params=_params(("parallel", "parallel"), 10 * S * bd * 4 + (12 << 20)),
        name="lru",
    )(p3, p3, cw, conv_b.reshape(1, width), wcat, bcat, lam8)


def _mixnorm_body(a_ref, b_ref, c_ref, g_ref, o_ref):
    off = 0
    for ref in (a_ref, b_ref, c_ref):
        if len(ref.shape) == 3:
            y = jnp.concatenate([ref[h, q * PITCH:q * PITCH + TN2, :] for q in range(2) for h in range(NT)],
                                axis=0)
        else:
            y = ref[...]
        w = y.shape[1]
        ms = jnp.mean(y * y, axis=-1, keepdims=True)
        o_ref[:, off:off + w] = (y * lax.rsqrt(ms + 1e-6) * g_ref[:, off:off + w]).astype(BF16)
        off += w


def _mixnorm(ya, yb_tiled, yc, gain):
    M = ya.shape[0]
    B, _, T, wb = yb_tiled.shape
    npair = T // (2 * PITCH)
    widths = (ya.shape[1], wb, yc.shape[1])
    D = sum(widths)
    tm = 2 * FFT_N2
    nat = lambda w: pl.BlockSpec((tm, w), lambda i: (i, 0))
    tiled = pl.BlockSpec((None, NT, None, 2 * PITCH, wb), lambda i: (i // npair, 0, i % npair, 0, 0))
    return pl.pallas_call(
        _mixnorm_body,
        grid=(M // tm,),
        in_specs=[nat(widths[0]), tiled, nat(widths[2]), pl.BlockSpec((1, D), lambda i: (0, 0))],
        out_specs=pl.BlockSpec((tm, D), lambda i: (i, 0)),
        out_shape=jax.ShapeDtypeStruct((M, D), BF16),
        compiler_params=_params(("parallel",), 32 << 20),
        name="mixnorm",
    )(ya, yb_tiled.reshape(B, NT, npair, 2 * PITCH, wb), yc, gain)


def kernel(x, ffa_w_gate, ffa_w_up, ffa_w_down, ffb_w_gate, ffb_w_up, ffb_w_down, ln_gain, ln_bias,
           w_in, w_out, mix_norm_gain, hy_conv_w, hy_conv_b, hy_filt_w1, hy_filt_b1, hy_filt_freq1,
           hy_filt_w2, hy_filt_b2, hy_filt_freq2, hy_filt_w3, hy_decay, hy_bias, lru_conv_w,
           lru_conv_b, lru_wa, lru_ba, lru_wx, lru_bx, lru_lambda):
    B, S, D = x.shape
    depth = w_in.shape[0]
    M = B * S
    alpha = (2 * depth) ** 0.25
    R = D // 2
    H = R // RET_HEAD_DIM
    Hy = D // 4
    Lw = D // 4
    col_hy = 4 * R
    col_lg = col_hy + 3 * Hy
    col_lx = col_lg + Lw

    pos = jnp.arange(S, dtype=F32)
    half = RET_HEAD_DIM // 2
    inv_freq = 1.0 / (10000.0 ** jnp.linspace(0.0, 1.0, half, dtype=F32))
    ang = pos[:, None] * inv_freq[None, :]
    cos, sin = jnp.cos(ang), jnp.sin(ang)
    log_g = jnp.log1p(-jnp.exp2(-5.0 - jnp.arange(H, dtype=F32)))
    lg_tab = jnp.broadcast_to(log_g[:, None, None], (H, 8, max(512, RET_HEAD_DIM)))
    tabs = _dft_tables(S)
    m = jnp.arange(2 * S)
    fpos = jnp.where(m < S, m, 2 * S - m).astype(F32)
    emb = hy_filt_w1.shape[1]
    bands = (emb - 1) // 2
    fr = jnp.linspace(1e-4, bands - 1, bands, dtype=F32)
    wang = 2.0 * math.pi * fpos / S
    zfeat = jnp.concatenate([(fpos / max(S - 1, 1))[:, None], jnp.cos(wang[:, None] * fr),
                             -jnp.sin(wang[:, None] * fr)], axis=-1)
    epad = LANES
    zfeat = jnp.pad(zfeat, ((0, 0), (0, epad - emb)))

    assert B == 2, "the Hyena DFT carries the two batch rows as one complex signal"
    ffa = (_pack_gate_up(ffa_w_gate, ffa_w_up), _to_bf16(ffa_w_down))
    ffb = (_pack_gate_up(ffb_w_gate, ffb_w_up), _to_bf16(ffb_w_down))
    w_in_b, w_out_b = _to_bf16(w_in), _to_bf16(w_out)
    row = lambda v: v.reshape(1, -1)

    h = x.reshape(M, D)
    for l in range(depth):
        h = _ffn(h, *ffa, l, row(ln_gain[l, 0]), row(ln_bias[l, 0]), alpha)
        p3 = _inproj(h, w_in_b, l).reshape(B, S, -1)
        y_ret = _retention(p3, cos, sin, lg_tab, H)
        w1p = jnp.pad(hy_filt_w1[l], ((0, epad - emb), (0, 0)))
        filt_params = (w1p, row(hy_filt_b1[l]), row(hy_filt_freq1[l]), hy_filt_w2[l],
                       row(hy_filt_b2[l]), row(hy_filt_freq2[l]), hy_filt_w3[l], row(hy_decay[l]))
        y_hy = _hyena(p3, col_hy, Hy, tabs, hy_conv_w[l], hy_conv_b[l], filt_params, hy_bias[l], zfeat)
        y_lru = _lru(p3, col_lg, col_lx, Lw, lru_conv_w[l], lru_conv_b[l], lru_wa[l], lru_ba[l],
                     lru_wx[l], lru_bx[l], lru_lambda[l])
        yb = _mixnorm(y_ret.reshape(M, R), y_hy, y_lru.reshape(M, Lw),
                      row(mix_norm_gain[l]))
        h = _outproj(yb, w_out_b, l, h, row(ln_gain[l, 1]), row(ln_bias[l, 1]), alpha)
        h = _ffn(h, *ffb, l, row(ln_gain[l, 2]), row(ln_bias[l, 2]), alpha)
    return h.reshape(B, S, D)
```

```python
import functools
import math

import numpy as np
import jax
import jax.numpy as jnp
from jax import lax
from jax.experimental import pallas as pl
from jax.experimental.pallas import tpu as pltpu

F32 = jnp.float32
BF16 = jnp.bfloat16

LANES = 128
RET_HEAD_DIM = 256
LRU_BLOCK_DIM = 128
LRU_C = 8.0
VMEM_CAP = 60 * 1024 * 1024
LN_ROWS = 16


def _params(semantics, vmem_bytes):
    return pltpu.CompilerParams(
        dimension_semantics=semantics,
        vmem_limit_bytes=int(min(VMEM_CAP, max(vmem_bytes, 16 * 1024 * 1024))))


def _layer_norm_rows(y, g, b, eps=1e-5):
    mu = jnp.mean(y, axis=-1, keepdims=True)
    d = y - mu
    var = jnp.mean(d * d, axis=-1, keepdims=True)
    return d * lax.rsqrt(var + eps) * g + b


def _silu(x):
    return x * jax.nn.sigmoid(x)


def _ffn_body(x_ref, wgu_a, wd_a, g_ref, b_ref, o_ref, xb_ref, *, alpha, rows, slab):
    j = pl.program_id(1)
    nj = pl.num_programs(1)
    tm = x_ref.shape[0]
    tf = wd_a.shape[0]
    D = o_ref.shape[1]

    @pl.when(j == 0)
    def _():
        xb_ref[...] = x_ref[...].astype(BF16)
        o_ref[...] = jnp.zeros_like(o_ref)

    def hidden(wgu_ref):
        gu = jnp.dot(xb_ref[...], wgu_ref[...], preferred_element_type=F32)
        return (_silu(gu[:, :tf]) * gu[:, tf:]).astype(BF16)

    def accumulate(pairs):
        hs = [(hidden(wgu_ref), wd_ref) for wgu_ref, wd_ref in pairs]
        for c0 in range(0, D, slab):
            part = None
            for h, wd_ref in hs:
                p = jnp.dot(h, wd_ref[:, c0:c0 + slab], preferred_element_type=F32)
                part = p if part is None else part + p
            o_ref[:, c0:c0 + slab] += part

    accumulate([(wgu_a, wd_a)])

    @pl.when(j == nj - 1)
    def _():
        g = g_ref[...]
        b = b_ref[...]

        def body(r, carry):
            sl = pl.ds(pl.multiple_of(r * rows, rows), rows)
            y = alpha * x_ref[sl, :] + 0.5 * o_ref[sl, :]
            o_ref[sl, :] = _layer_norm_rows(y, g, b)
            return carry

        lax.fori_loop(0, tm // rows, body, 0, unroll=2)


def _cast_body(x_ref, o_ref):
    o_ref[...] = x_ref[...].astype(BF16)


def _to_bf16(w):
    L, R, C = w.shape
    tr = 256
    while tr > 8 and (R % tr or tr * C * 4 > (12 << 20)):
        tr //= 2
    spec = pl.BlockSpec((None, tr, C), lambda l, i: (l, i, 0))
    return pl.pallas_call(
        _cast_body,
        grid=(L, R // tr),
        in_specs=[spec],
        out_specs=spec,
        out_shape=jax.ShapeDtypeStruct(w.shape, BF16),
        compiler_params=_params(("parallel", "parallel"), 3 * tr * C * 4 + (8 << 20)),
        name="to_bf16",
    )(w)


FFN_TF = 256


def _pack_gate_up_body(g_ref, u_ref, o_ref, *, tf):
    for c in range(g_ref.shape[1] // tf):
        o_ref[c, :, :tf] = g_ref[:, c * tf:(c + 1) * tf].astype(BF16)
        o_ref[c, :, tf:] = u_ref[:, c * tf:(c + 1) * tf].astype(BF16)


def _pack_gate_up(wg, wu):
    L, D, F = wg.shape
    tf = FFN_TF if F % FFN_TF == 0 else F
    tr = min(128, D)
    spec = pl.BlockSpec((None, tr, F), lambda l, i: (l, i, 0))
    return pl.pallas_call(
        functools.partial(_pack_gate_up_body, tf=tf),
        grid=(L, D // tr),
        in_specs=[spec, spec],
        out_specs=pl.BlockSpec((None, F // tf, tr, 2 * tf), lambda l, i: (l, 0, i, 0)),
        out_shape=jax.ShapeDtypeStruct((L, F // tf, D, 2 * tf), BF16),
        compiler_params=_params(("parallel", "parallel"), 6 * tr * F * 4 + (8 << 20)),
        name="pack_gate_up",
    )(wg, wu)


def _ffn(x, wgu, wd, layer, g, b, alpha):
    M, D = x.shape
    F = wd.shape[1]
    tm = min(1024, M)
    tf = FFN_TF if F % FFN_TF == 0 else F
    nf = F // tf
    rows = min(LN_ROWS, tm)
    slab = 512 if D % 512 == 0 else D
    once = pl.Buffered(1)
    vmem = (tm * D * 4 + tm * D * 2 + tm * D * 4 + 2 * 3 * D * tf * 2 + 6 * tm * tf * 4
            + 2 * tm * slab * 4)
    return pl.pallas_call(
        functools.partial(_ffn_body, alpha=alpha, rows=rows, slab=slab),
        grid=(M // tm, nf),
        in_specs=[
            pl.BlockSpec((tm, D), lambda i, j: (i, 0), pipeline_mode=once),
            pl.BlockSpec((None, None, D, 2 * tf), lambda i, j: (layer, j, 0, 0)),
            pl.BlockSpec((None, tf, D), lambda i, j: (layer, j, 0)),
            pl.BlockSpec((1, D), lambda i, j: (0, 0)),
            pl.BlockSpec((1, D), lambda i, j: (0, 0)),
        ],
        out_specs=pl.BlockSpec((tm, D), lambda i, j: (i, 0), pipeline_mode=once),
        out_shape=jax.ShapeDtypeStruct((M, D), F32),
        scratch_shapes=[pltpu.VMEM((tm, D), BF16)],
        compiler_params=_params(("parallel", "arbitrary"), vmem + (4 << 20)),
        name="ffn",
    )(x, wgu, wd, g, b)


def _inproj_body(x_ref, w_ref, o_ref, xb_ref):
    @pl.when(pl.program_id(1) == 0)
    def _():
        xb_ref[...] = x_ref[...].astype(BF16)

    o_ref[...] = jnp.dot(xb_ref[...], w_ref[...], preferred_element_type=F32)


def _inproj(x, w, layer):
    M, K = x.shape
    N = w.shape[2]
    tm = min(1024, M)
    tn = 1024 if N % 1024 == 0 else (512 if N % 512 == 0 else 128)
    vmem = tm * K * 4 + tm * K * 2 + 2 * K * tn * 2 + 3 * tm * tn * 4
    return pl.pallas_call(
        _inproj_body,
        grid=(M // tm, N // tn),
        in_specs=[
            pl.BlockSpec((tm, K), lambda i, j: (i, 0), pipeline_mode=pl.Buffered(1)),
            pl.BlockSpec((None, K, tn), lambda i, j: (layer, 0, j)),
        ],
        out_specs=pl.BlockSpec((tm, tn), lambda i, j: (i, j)),
        out_shape=jax.ShapeDtypeStruct((M, N), F32),
        scratch_shapes=[pltpu.VMEM((tm, K), BF16)],
        compiler_params=_params(("parallel", "arbitrary"), vmem + (4 << 20)),
        name="inproj",
    )(x, w)


def _outproj_body(y_ref, w_ref, x_ref, g_ref, b_ref, o_ref, *, alpha, rows, nj, slab):
    j = pl.program_id(1)
    tm, D = o_ref.shape
    rc = x_ref.shape[0]

    @pl.when(j == 0)
    def _():
        o_ref[...] = jnp.zeros_like(o_ref)

    yk = y_ref[...]
    for c0 in range(0, D, slab):
        o_ref[:, c0:c0 + slab] += jnp.dot(yk, w_ref[:, c0:c0 + slab], preferred_element_type=F32)
    sl = pl.ds(pl.multiple_of(j * rc, rc), rc)
    o_ref[sl, :] += alpha * x_ref[...]

    @pl.when(j == nj - 1)
    def _():
        g = g_ref[...]
        b = b_ref[...]

        def body(r, carry):
            rs = pl.ds(pl.multiple_of(r * rows, rows), rows)
            o_ref[rs, :] = _layer_norm_rows(o_ref[rs, :], g, b)
            return carry

        lax.fori_loop(0, tm // rows, body, 0, unroll=2)


def _outproj(y, w, layer, x, g, b, alpha):
    M, K = y.shape
    D = w.shape[2]
    tm = min(512, M)
    tk = 1024 if K % 1024 == 0 else K
    nj = K // tk
    rc = tm // nj
    rows = min(LN_ROWS, tm)
    slab = 1024 if D % 1024 == 0 else D
    vmem = 2 * tm * tk * 2 + 2 * tk * D * 2 + 2 * rc * D * 4 + 2 * tm * D * 4 + 2 * tm * slab * 4
    return pl.pallas_call(
        functools.partial(_outproj_body, alpha=alpha, rows=rows, nj=nj, slab=slab),
        grid=(M // tm, nj),
        in_specs=[
            pl.BlockSpec((tm, tk), lambda i, j: (i, j)),
            pl.BlockSpec((None, tk, D), lambda i, j: (layer, j, 0)),
            pl.BlockSpec((rc, D), lambda i, j: (i * nj + j, 0)),
            pl.BlockSpec((1, D), lambda i, j: (0, 0)),
            pl.BlockSpec((1, D), lambda i, j: (0, 0)),
        ],
        out_specs=pl.BlockSpec((tm, D), lambda i, j: (i, 0)),
        out_shape=jax.ShapeDtypeStruct((M, D), F32),
        compiler_params=_params(("parallel", "arbitrary"), vmem + (6 << 20)),
        name="outproj",
    )(y, w, x, g, b)


def _rotary(x, cos, sin):
    half = x.shape[-1] // 2
    x1 = x[:, :half]
    x2 = x[:, half:]
    return jnp.concatenate([x1 * cos - x2 * sin, x1 * sin + x2 * cos], axis=-1)


def _ret_tables(cos_ref, sin_ref, chunk, C):
    rows = pl.ds(pl.multiple_of(chunk * C, C), C)
    return cos_ref[rows, :], sin_ref[rows, :]


def _row_index(shape):
    return lax.broadcasted_iota(jnp.int32, shape, 0).astype(F32)


def _ret_fwd_body(q_ref, k_ref, v_ref, cos_ref, sin_ref, lg_ref, y_ref,
                  state_ref, qd_ref, kd_ref, intra_ref):
    C = q_ref.shape[0]
    dk = RET_HEAD_DIM
    heads = lg_ref.shape[0]

    @pl.when(pl.program_id(2) == 0)
    def _():
        state_ref[...] = jnp.zeros_like(state_ref)
        row = _row_index((C, dk))
        ii = lax.broadcasted_iota(jnp.int32, (C, C), 0)
        jj = lax.broadcasted_iota(jnp.int32, (C, C), 1)
        dist = jnp.abs(ii - jj).astype(F32)
        for t in range(heads):
            lg = lg_ref[t, 0:1, :dk]
            qd_ref[t] = jnp.exp(lg * (row + 1.0))
            kd_ref[t] = jnp.exp(lg * (C - 1.0 - row))
            intra_ref[t] = jnp.exp(lg_ref[t, 0:1, :C] * dist)

    cos, sin = _ret_tables(cos_ref, sin_ref, pl.program_id(2), C)
    for t in range(heads):
        cols = slice(t * dk, (t + 1) * dk)
        lg = lg_ref[t, 0:1, :dk]
        q = _rotary(q_ref[:, cols], cos, sin)
        k = _rotary(k_ref[:, cols], cos, sin) * (dk ** -0.5)
        vb = v_ref[:, cols].astype(BF16)
        q_in = (q * qd_ref[t]).astype(BF16)
        k_out = (k * kd_ref[t]).astype(BF16)
        s = lax.dot_general(q.astype(BF16), k.astype(BF16), (((1,), (1,)), ((), ())),
                            preferred_element_type=F32) * intra_ref[t]
        state = state_ref[t]
        y = jnp.dot(s.astype(BF16), vb, preferred_element_type=F32)
        y_ref[:, cols] = y + jnp.dot(q_in, state.astype(BF16), preferred_element_type=F32)
        kv = lax.dot_general(k_out, vb, (((0,), (0,)), ((), ())), preferred_element_type=F32)
        state_ref[t] = state * jnp.exp(lg * float(C)) + kv


def _ret_bwd_body(q_ref, k_ref, v_ref, g_ref, cos_ref, sin_ref, lg_ref, yf_ref, y_ref,
                  state_ref, qd_ref, kd_ref):
    C = q_ref.shape[0]
    dk = RET_HEAD_DIM
    heads = lg_ref.shape[0]

    @pl.when(pl.program_id(2) == 0)
    def _():
        state_ref[...] = jnp.zeros_like(state_ref)
        row = _row_index((C, dk))
        for t in range(heads):
            lg = lg_ref[t, 0:1, :dk]
            qd_ref[t] = jnp.exp(lg * (C - row))
            kd_ref[t] = jnp.exp(lg * row)

    cos, sin = _ret_tables(cos_ref, sin_ref, pl.num_programs(2) - 1 - pl.program_id(2), C)
    for t in range(heads):
        cols = slice(t * dk, (t + 1) * dk)
        lg = lg_ref[t, 0:1, :dk]
        q = _rotary(q_ref[:, cols], cos, sin)
        k = _rotary(k_ref[:, cols], cos, sin) * (dk ** -0.5)
        vb = v_ref[:, cols].astype(BF16)
        q_in = (q * qd_ref[t]).astype(BF16)
        k_out = (k * kd_ref[t]).astype(BF16)
        state = state_ref[t]
        y = yf_ref[:, cols] + jnp.dot(q_in, state.astype(BF16), preferred_element_type=F32)
        kv = lax.dot_general(k_out, vb, (((0,), (0,)), ((), ())), preferred_element_type=F32)
        state_ref[t] = state * jnp.exp(lg * float(C)) + kv
        mu = jnp.mean(y, axis=-1, keepdims=True)
        d = y - mu
        var = jnp.mean(d * d, axis=-1, keepdims=True)
        y_ref[:, cols] = _silu(g_ref[:, cols]) * (d * lax.rsqrt(var + 1e-5))


def _retention(p3, cos, sin, lg_tab, H):
    B, S, _ = p3.shape
    dk = RET_HEAD_DIM
    C = min(512, S)
    nc = S // C
    W = lg_tab.shape[-1]
    hp = 2 if H % 2 == 0 else 1
    G = H // hp
    wb = hp * dk
    blk = lambda off, rev: pl.BlockSpec(
        (None, C, wb),
        (lambda b, g, c: (b, nc - 1 - c, off // hp + g)) if rev else (lambda b, g, c: (b, c, off // hp + g)))
    tab = pl.BlockSpec((S, dk // 2), lambda b, g, c: (0, 0))
    lg_spec = pl.BlockSpec((hp, 8, W), lambda b, g, c: (g, 0, 0))
    vmem = 40 << 20
    tables = [pltpu.VMEM((hp, dk, dk), F32), pltpu.VMEM((hp, C, dk), F32), pltpu.VMEM((hp, C, dk), F32)]
    y_f = pl.pallas_call(
        _ret_fwd_body,
        grid=(B, G, nc),
        in_specs=[blk(0, False), blk(H, False), blk(2 * H, False), tab, tab, lg_spec],
        out_specs=blk(0, False),
        out_shape=jax.ShapeDtypeStruct((B, S, H * dk), F32),
        scratch_shapes=tables + [pltpu.VMEM((hp, C, C), F32)],
        compiler_params=_params(("parallel", "parallel", "arbitrary"), vmem),
        name="ret_fwd",
    )(p3, p3, p3, cos, sin, lg_tab)
    return pl.pallas_call(
        _ret_bwd_body,
        grid=(B, G, nc),
        in_specs=[blk(0, True), blk(H, True), blk(2 * H, True), blk(3 * H, True),
                  tab, tab, lg_spec, blk(0, True)],
        out_specs=blk(0, True),
        out_shape=jax.ShapeDtypeStruct((B, S, H * dk), F32),
        scratch_shapes=tables,
        compiler_params=_params(("parallel", "parallel", "arbitrary"), vmem),
        name="ret_bwd",
    )(p3, p3, p3, p3, cos, sin, lg_tab, y_f)


HALO = 8


def _fill_haloed(xs_ref, x_ref, S):
    zeros = jnp.zeros((HALO, xs_ref.shape[1]), F32)
    xs_ref[0:HALO, :] = zeros
    xs_ref[S + HALO:S + 2 * HALO, :] = zeros
    xs_ref[HALO:S + HALO, :] = x_ref[...]


def _conv_rows(xs_ref, r0, R, taps, left, bias):
    ext = xs_ref[pl.ds(r0, R + 2 * HALO), :]
    n = R + 2 * HALO
    acc = None
    for j, w in enumerate(taps):
        d = j - left
        sh = ext if d == 0 else pltpu.roll(ext, (-d) % n, axis=0)
        term = w * sh[HALO:HALO + R, :]
        acc = term if acc is None else acc + term
    return acc + bias


FFT_N2 = 128
TN2 = 32
NT = FFT_N2 // TN2
PITCH = TN2 + 4
STAGE_CB = LANES


def _store_tiled_pair(o_ref, lead, blocks):
    pad = jnp.zeros((PITCH - TN2, blocks[0].shape[1]), F32)
    for h in range(NT):
        for q, block in enumerate(blocks):
            o_ref[(h, *lead, slice(q * PITCH, q * PITCH + TN2))] = block[h * TN2:(h + 1) * TN2, :]
            o_ref[(h, *lead, slice(q * PITCH + TN2, (q + 1) * PITCH))] = pad


def _hy_conv_body(u_ref, w_ref, b_ref, o_ref, xs_ref, *, S, R, K):
    _fill_haloed(xs_ref, u_ref, S)
    taps = [w_ref[j:j + 1, :] for j in range(K)]
    bias = b_ref[...]

    def body(i, carry):
        r0 = pl.multiple_of(i * R, R)
        res = _conv_rows(xs_ref, r0, R, taps, (K - 1) // 2, bias)
        _store_tiled_pair(o_ref, (i,), [res[:FFT_N2], res[FFT_N2:]])
        return carry

    lax.fori_loop(0, S // R, body, 0)


def _hy_conv(p3, col0, width, w, b):
    B, S, _ = p3.shape
    K = w.shape[0]
    cb = LANES
    nb = width // cb
    C = width // 3
    R = 2 * FFT_N2
    npair = S // R
    wp = jnp.zeros((8, width), F32).at[:K].set(w)
    off = col0 // cb
    per = C // cb
    return pl.pallas_call(
        functools.partial(_hy_conv_body, S=S, R=R, K=K),
        grid=(B, nb),
        in_specs=[
            pl.BlockSpec((None, S, cb), lambda b_, j: (b_, 0, off + j)),
            pl.BlockSpec((8, cb), lambda b_, j: (0, j)),
            pl.BlockSpec((1, cb), lambda b_, j: (0, j)),
        ],
        out_specs=pl.BlockSpec((None, NT, npair, 2 * PITCH, cb),
                               lambda b_, j: ((j // per) * B + b_, 0, 0, 0, j % per)),
        out_shape=jax.ShapeDtypeStruct((3 * B, NT, npair, 2 * PITCH, C), F32),
        scratch_shapes=[pltpu.VMEM((S + 2 * HALO, cb), F32)],
        compiler_params=_params(("parallel", "parallel"), 7 * S * cb * 4 + (8 << 20)),
        name="hy_conv",
    )(p3, wp, b.reshape(1, width)).reshape(3 * B, NT, npair * 2 * PITCH, C)


def _dot_split(a, b):
    a_hi = a.astype(BF16).astype(F32)
    b_hi = b.astype(BF16).astype(F32)
    lhs = jnp.concatenate([a_hi, a_hi, a - a_hi], axis=1).astype(BF16)
    rhs = jnp.concatenate([b_hi, b - b_hi, b_hi], axis=0).astype(BF16)
    return jnp.dot(lhs, rhs, preferred_element_type=F32)


def _hy_filter_body(z_ref, w1_ref, b1_ref, f1_ref, w2_ref, b2_ref, f2_ref, w3_ref, dec_ref,
                    o_ref, sum_ref, *, L, tr):
    i = pl.program_id(0)
    z = z_ref[...]
    h = jnp.sin(f1_ref[...] * (_dot_split(z, w1_ref[...]) + b1_ref[...]))
    h = jnp.sin(f2_ref[...] * (_dot_split(h, w2_ref[...]) + b2_ref[...]))
    h = _dot_split(h, w3_ref[...])
    t = z[:, 0:1]
    h = h * jnp.exp(-t * jnp.abs(dec_ref[...]))
    m = i * tr + lax.broadcasted_iota(jnp.int32, h.shape, 0)
    filt = jnp.where(m == L, 0.0, h)
    _store_tiled_pair(o_ref, (), [filt[:FFT_N2], filt[FFT_N2:]])

    @pl.when(i == 0)
    def _():
        sum_ref[...] = jnp.zeros_like(sum_ref)

    sum_ref[...] += jnp.sum(jnp.abs(filt), axis=0, keepdims=True)


def _hy_filter(zfeat, w1, b1, f1, w2, b2, f2, w3, dec, L):
    N, E = zfeat.shape
    hid = w1.shape[1]
    C4 = w3.shape[1]
    tr = 2 * FFT_N2
    nt_half = L // tr
    full = lambda a: pl.BlockSpec(a.shape, lambda i: (0,) * a.ndim)
    side = lambda a: pl.BlockSpec((a.shape[0], C4 // 2), lambda i: (0, i // nt_half))
    args = (w1, b1, f1, w2, b2, f2, w3, dec)
    return pl.pallas_call(
        functools.partial(_hy_filter_body, L=L, tr=tr),
        grid=(N // tr,),
        in_specs=[pl.BlockSpec((tr, E), lambda i: (i, 0))] + [full(a) for a in args[:6]]
        + [side(w3), side(dec)],
        out_specs=[pl.BlockSpec((NT, None, 2 * PITCH, C4 // 2), lambda i: (0, i, 0, 0)),
                   pl.BlockSpec((1, C4 // 2), lambda i: (0, 0))],
        out_shape=[jax.ShapeDtypeStruct((NT, N // tr, 2 * PITCH, C4 // 2), F32),
                   jax.ShapeDtypeStruct((1, C4 // 2), F32)],
        compiler_params=_params(("arbitrary",), 32 << 20),
        name="hy_filter",
    )(zfeat, *args)


def _stage_a_body(w_ref, x_ref, o_ref, *, nparts, rows):
    mo = w_ref.shape[0]
    w = w_ref[...]

    def body(r, carry):
        parts = [x_ref[p, pl.ds(r, rows, stride=PITCH), :] for p in range(nparts)]
        x = parts[0] if nparts == 1 else jnp.concatenate(parts, axis=0)
        o_ref[pl.ds(r, mo, stride=PITCH), :] = jnp.dot(w, x.astype(BF16), preferred_element_type=F32)
        return carry

    lax.fori_loop(0, TN2, body, 0, unroll=8)
    pad = jnp.zeros((mo, o_ref.shape[1]), F32)
    for r in range(TN2, PITCH):
        o_ref[pl.ds(r, mo, stride=PITCH), :] = pad


def _stage_a(wmat, x, group, nparts):
    mo, K = wmat.shape
    _, _, T, C = x.shape
    rows = T // PITCH
    assert K == nparts * rows
    cb = min(STAGE_CB, C)
    vmem = 2 * nparts * T * cb * 4 + 2 * mo * PITCH * cb * 4 + (12 << 20)
    return pl.pallas_call(
        functools.partial(_stage_a_body, nparts=nparts, rows=rows),
        grid=(C // cb, NT),
        in_specs=[pl.BlockSpec((mo, K), lambda j, h: (0, 0)),
                  pl.BlockSpec((nparts, None, T, cb), lambda j, h: (group, h, 0, j))],
        out_specs=pl.BlockSpec((None, mo * PITCH, cb), lambda j, h: (h, 0, j)),
        out_shape=jax.ShapeDtypeStruct((NT, mo * PITCH, C), F32),
        compiler_params=_params(("parallel", "parallel"), vmem),
        name="hy_stage_a",
    )(wmat, x)


def _hy_freq_body(g_ref, gi_ref, x_ref, f_ref, s_ref, o_ref, *, kb):
    n2 = FFT_N2
    inv = 1.0 / (s_ref[...] + 1e-6)

    def gather(ref, t):
        lo = (t % 2) * PITCH
        return jnp.concatenate([ref[h, ri, t // 2, lo:lo + TN2, :] for ri in range(2) for h in range(NT)],
                               axis=0)

    pad = jnp.zeros((PITCH - TN2, o_ref.shape[-1]), F32)
    for t in range(kb):
        g = g_ref[t]
        y = jnp.dot(g, gather(x_ref, t).astype(BF16), preferred_element_type=F32)
        h = jnp.dot(g, gather(f_ref, t).astype(BF16), preferred_element_type=F32) * inv
        yr, yi = y[:n2], y[n2:]
        hr, hi = h[:n2], h[n2:]
        z = jnp.concatenate([yr * hr - yi * hi, yr * hi + yi * hr], axis=0).astype(BF16)
        e = jnp.dot(gi_ref[t], z, preferred_element_type=F32)
        lo = (t % 2) * PITCH
        for ri in range(2):
            for hh in range(NT):
                o_ref[hh, ri, t // 2, lo:lo + TN2, :] = e[ri * n2 + hh * TN2:ri * n2 + (hh + 1) * TN2]
                o_ref[hh, ri, t // 2, lo + TN2:lo + PITCH, :] = pad


def _hy_freq(gmat, gimat, xa, fa, asum, order, C):
    N1 = 2 * xa.shape[2]
    n2 = FFT_N2
    kb = min(4, N1)
    blk = pl.BlockSpec((NT, 2, kb // 2, 2 * PITCH, C), lambda i: (0, 0, i, 0, 0))
    gspec = pl.BlockSpec((kb, 2 * n2, 2 * n2), lambda i: (i, 0, 0))
    return pl.pallas_call(
        functools.partial(_hy_freq_body, kb=kb),
        grid=(N1 // kb,),
        in_specs=[gspec, gspec, blk,
                  pl.BlockSpec((NT, 2, kb // 2, 2 * PITCH, C), lambda i: (0, 0, i, 0, order)),
                  pl.BlockSpec((1, C), lambda i: (0, order))],
        out_specs=blk,
        out_shape=jax.ShapeDtypeStruct(xa.shape, F32),
        compiler_params=_params(("parallel",), 6 * 2 * kb * n2 * C * 4 + 24 * n2 * C * 4 + (8 << 20)),
        name="hy_freq",
    )(gmat, gimat, xa, fa, asum)


def _stage_d_body(w_ref, e_ref, z_ref, x_ref, b_ref, o_ref, *, B, nh):
    w = w_ref[...]
    bias = b_ref[...]
    mi = w.shape[1]

    def body(r, carry):
        e = e_ref[pl.ds(r, mi, stride=PITCH), :].astype(BF16)
        conv = jnp.dot(w, e, preferred_element_type=F32)
        rows = pl.ds(r, nh, stride=PITCH)
        for b in range(B):
            o_ref[b, rows, :] = x_ref[b, rows, :] * (conv[b * nh:(b + 1) * nh] + bias * z_ref[b, rows, :])
        return carry

    lax.fori_loop(0, TN2, body, 0, unroll=8)
    pad = jnp.zeros((nh, o_ref.shape[-1]), F32)
    for r in range(TN2, PITCH):
        for b in range(B):
            o_ref[b, pl.ds(r, nh, stride=PITCH), :] = pad


def _stage_d(w4, e, z, z_group, gate, gate_group, bias, B):
    mo, mi = w4.shape
    nh = mo // B
    T, C = z.shape[2], z.shape[3]
    cb = min(STAGE_CB, C)
    nat = lambda grp: pl.BlockSpec((B, None, T, cb), lambda j, h: (grp, h, 0, j))
    vmem = 2 * mi * PITCH * cb * 4 + 6 * B * T * cb * 4 + (12 << 20)
    return pl.pallas_call(
        functools.partial(_stage_d_body, B=B, nh=nh),
        grid=(C // cb, NT),
        in_specs=[pl.BlockSpec((mo, mi), lambda j, h: (0, 0)),
                  pl.BlockSpec((None, mi * PITCH, cb), lambda j, h: (h, 0, j)),
                  nat(z_group), nat(gate_group),
                  pl.BlockSpec((1, cb), lambda j, h: (0, j))],
        out_specs=nat(0),
        out_shape=jax.ShapeDtypeStruct((B, NT, T, C), F32),
        compiler_params=_params(("parallel", "parallel"), vmem),
        name="hy_stage_d",
    )(w4, e, z, gate, bias)


def _dft_tables(S):
    N = 2 * S
    n2 = FFT_N2
    N1 = N // n2
    nh = S // n2
    k1 = np.arange(N1)
    f1 = np.exp(-2j * np.pi * np.outer(k1, np.arange(N1)) / N1)
    fa = f1[:, :nh]
    wa = np.block([[fa.real, -fa.imag], [fa.imag, fa.real]])
    wa_real = np.concatenate([f1.real, f1.imag], axis=0)
    a2 = np.arange(n2)
    base = np.exp(-2j * np.pi * np.outer(a2, a2) / n2)
    tw = np.exp(-2j * np.pi * np.outer(k1, a2) / N)
    g = base[None, :, :] * tw[:, None, :]
    gi = np.conj(np.transpose(g, (0, 2, 1)))
    blockify = lambda c: np.concatenate(
        [np.concatenate([c.real, -c.imag], axis=-1), np.concatenate([c.imag, c.real], axis=-1)], axis=-2)
    f4 = np.exp(2j * np.pi * np.outer(np.arange(nh), k1) / N1) / N
    wd = np.block([[f4.real, -f4.imag], [f4.imag, f4.real]])
    cast = lambda a: jnp.asarray(a, dtype=F32).astype(BF16)
    return dict(N1=N1, nh=nh, wa=cast(wa), wa_real=cast(wa_real), g=cast(blockify(g)),
                gi=cast(blockify(gi)), wd=cast(wd))


def _hyena(p3, col0, C, tabs, conv_w, conv_b, filt_params, hy_bias, zfeat):
    B, S, _ = p3.shape
    N1 = tabs["N1"]
    u = _hy_conv(p3, col0, 3 * C, conv_w, conv_b)
    filt, asum = _hy_filter(zfeat, *filt_params, S)
    filt = filt.reshape(1, NT, N1 * PITCH, 2 * C)
    fa = _stage_a(tabs["wa_real"], filt, 0, 1).reshape(NT, 2, N1 // 2, 2 * PITCH, 2 * C)
    z = u
    for order in range(2):
        xa = _stage_a(tabs["wa"], z, 0, B)
        e = _hy_freq(tabs["g"], tabs["gi"], xa.reshape(NT, 2, N1 // 2, 2 * PITCH, C), fa, asum, order, C)
        z = _stage_d(tabs["wd"], e.reshape(NT, 2 * N1 * PITCH, C), z, 0, u, 1 + order,
                     hy_bias[order].reshape(1, C), B)
    return z


def _scan_rows(a, b, reverse):
    R = a.shape[0]
    row = lax.broadcasted_iota(jnp.int32, a.shape, 0)
    d = 1
    while d < R:
        if reverse:
            a_s = pltpu.roll(a, R - d, axis=0)
            b_s = pltpu.roll(b, R - d, axis=0)
            m = row < R - d
        else:
            a_s = pltpu.roll(a, d, axis=0)
            b_s = pltpu.roll(b, d, axis=0)
            m = row >= d
        b = jnp.where(m, a * b_s + b, b)
        a = jnp.where(m, a * a_s, a)
        d *= 2
    return a, b


def _gelu_tanh(x):
    return 0.5 * x * (1.0 + jnp.tanh(0.7978845608028654 * (x + 0.044715 * x * x * x)))


def _lru_body(g_ref, x_ref, cw_ref, cb_ref, w_ref, bias_ref, lam_ref, o_ref,
              xs_ref, hf_ref, ab_ref, bb_ref, *, S, R, K):
    bd = x_ref.shape[1]
    _fill_haloed(xs_ref, x_ref, S)
    taps = [cw_ref[j:j + 1, :] for j in range(K)]
    cbias = cb_ref[...]
    w = w_ref[...]
    bias = bias_ref[...]
    nl = -lam_ref[...]
    sp = jnp.maximum(nl, 0.0) + jnp.log1p(jnp.exp(-jnp.abs(nl)))
    sp0 = sp[0:1, :]
    sp1 = sp[1:2, :]
    nchunk = S // R

    def gates(r, i, spd, xr):
        log_a = -LRU_C * jax.nn.sigmoid(r) * spd
        a = jnp.exp(log_a)
        b = jnp.sqrt(1.0 - a * a) * (jax.nn.sigmoid(i) * xr)
        return a, b

    def fwd(i, hc):
        r0 = pl.multiple_of(i * R, R)
        xr = _conv_rows(xs_ref, r0, R, taps, (K - 1) // 2, cbias)
        proj = jnp.dot(xr.astype(BF16), w, preferred_element_type=F32) + bias
        a0, b0 = gates(proj[:, 0:bd], proj[:, bd:2 * bd], sp0, xr)
        a1, b1 = gates(proj[:, 2 * bd:3 * bd], proj[:, 3 * bd:4 * bd], sp1, xr)
        ab_ref[pl.ds(r0, R), :] = a1
        bb_ref[pl.ds(r0, R), :] = b1
        ac, hl = _scan_rows(a0, b0, reverse=False)
        h = ac * hc + hl
        hf_ref[pl.ds(r0, R), :] = h
        return h[R - 1:R, :]

    lax.fori_loop(0, nchunk, fwd, jnp.zeros((1, bd), F32))

    def bwd(t, hc):
        r0 = pl.multiple_of((nchunk - 1 - t) * R, R)
        sl = pl.ds(r0, R)
        ac, hl = _scan_rows(ab_ref[sl, :], bb_ref[sl, :], reverse=True)
        h = ac * hc + hl
        o_ref[sl, :] = _gelu_tanh(g_ref[sl, :]) * (hf_ref[sl, :] + h)
        return h[0:1, :]

    lax.fori_loop(0, nchunk, bwd, jnp.zeros((1, bd), F32))


def _lru(p3, col_g, col_x, width, conv_w, conv_b, wa, ba, wx, bx, lam):
    B, S, _ = p3.shape
    bd = LRU_BLOCK_DIM
    nb = width // bd
    K = conv_w.shape[0]
    R = min(256, S)
    cw = jnp.zeros((8, width), F32).at[:K].set(conv_w)
    wcat = jnp.concatenate([wa[0], wx[0], wa[1], wx[1]], axis=-1).astype(BF16)
    bsplit = lambda v: v.reshape(nb, 1, bd)
    bcat = jnp.concatenate([bsplit(ba[0]), bsplit(bx[0]), bsplit(ba[1]), bsplit(bx[1])], axis=-1)
    lam8 = jnp.zeros((8, width), F32).at[:2].set(lam)
    og, ox = col_g // bd, col_x // bd
    return pl.pallas_call(
        functools.partial(_lru_body, S=S, R=R, K=K),
        grid=(B, nb),
        in_specs=[
            pl.BlockSpec((None, S, bd), lambda b, j: (b, 0, og + j)),
            pl.BlockSpec((None, S, bd), lambda b, j: (b, 0, ox + j)),
            pl.BlockSpec((8, bd), lambda b, j: (0, j)),
            pl.BlockSpec((1, bd), lambda b, j: (0, j)),
            pl.BlockSpec((None, bd, 4 * bd), lambda b, j: (j, 0, 0)),
            pl.BlockSpec((None, 1, 4 * bd), lambda b, j: (j, 0, 0)),
            pl.BlockSpec((8, bd), lambda b, j: (0, j)),
        ],
        out_specs=pl.BlockSpec((None, S, bd), lambda b, j: (b, 0, j)),
        out_shape=jax.ShapeDtypeStruct((B, S, width), F32),
        scratch_shapes=[pltpu.VMEM((S + 2 * HALO, bd), F32), pltpu.VMEM((S, bd), F32),
                        pltpu.VMEM((S, bd), F32), pltpu.VMEM((S, bd), F32)],
        compiler_params=_params(("parallel", "parallel"), 10 * S * bd * 4 + (12 << 20)),
        name="lru",
    )(p3, p3, cw, conv_b.reshape(1, width), wcat, bcat, lam8)


def _mixnorm_body(a_ref, b_ref, c_ref, g_ref, o_ref):
    off = 0
    for ref in (a_ref, b_ref, c_ref):
        if len(ref.shape) == 3:
            y = jnp.concatenate([ref[h, q * PITCH:q * PITCH + TN2, :] for q in range(2) for h in range(NT)],
                                axis=0)
        else:
            y = ref[...]
        w = y.shape[1]
        ms = jnp.mean(y * y, axis=-1, keepdims=True)
        o_ref[:, off:off + w] = (y * lax.rsqrt(ms + 1e-6) * g_ref[:, off:off + w]).astype(BF16)
        off += w


def _mixnorm(ya, yb_tiled, yc, gain):
    M = ya.shape[0]
    B, _, T, wb = yb_tiled.shape
    npair = T // (2 * PITCH)
    widths = (ya.shape[1], wb, yc.shape[1])
    D = sum(widths)
    tm = 2 * FFT_N2
    nat = lambda w: pl.BlockSpec((tm, w), lambda i: (i, 0))
    tiled = pl.BlockSpec((None, NT, None, 2 * PITCH, wb), lambda i: (i // npair, 0, i % npair, 0, 0))
    return pl.pallas_call(
        _mixnorm_body,
        grid=(M // tm,),
        in_specs=[nat(widths[0]), tiled, nat(widths[2]), pl.BlockSpec((1, D), lambda i: (0, 0))],
        out_specs=pl.BlockSpec((tm, D), lambda i: (i, 0)),
        out_shape=jax.ShapeDtypeStruct((M, D), BF16),
        compiler_params=_params(("parallel",), 32 << 20),
        name="mixnorm",
    )(ya, yb_tiled.reshape(B, NT, npair, 2 * PITCH, wb), yc, gain)


def kernel(x, ffa_w_gate, ffa_w_up, ffa_w_down, ffb_w_gate, ffb_w_up, ffb_w_down, ln_gain, ln_bias,
           w_in, w_out, mix_norm_gain, hy_conv_w, hy_conv_b, hy_filt_w1, hy_filt_b1, hy_filt_freq1,
           hy_filt_w2, hy_filt_b2, hy_filt_freq2, hy_filt_w3, hy_decay, hy_bias, lru_conv_w,
           lru_conv_b, lru_wa, lru_ba, lru_wx, lru_bx, lru_lambda):
    B, S, D = x.shape
    depth = w_in.shape[0]
    M = B * S
    alpha = (2 * depth) ** 0.25
    R = D // 2
    H = R // RET_HEAD_DIM
    Hy = D // 4
    Lw = D // 4
    col_hy = 4 * R
    col_lg = col_hy + 3 * Hy
    col_lx = col_lg + Lw

    pos = jnp.arange(S, dtype=F32)
    half = RET_HEAD_DIM // 2
    inv_freq = 1.0 / (10000.0 ** jnp.linspace(0.0, 1.0, half, dtype=F32))
    ang = pos[:, None] * inv_freq[None, :]
    cos, sin = jnp.cos(ang), jnp.sin(ang)
    log_g = jnp.log1p(-jnp.exp2(-5.0 - jnp.arange(H, dtype=F32)))
    lg_tab = jnp.broadcast_to(log_g[:, None, None], (H, 8, max(512, RET_HEAD_DIM)))
    tabs = _dft_tables(S)
    m = jnp.arange(2 * S)
    fpos = jnp.where(m < S, m, 2 * S - m).astype(F32)
    emb = hy_filt_w1.shape[1]
    bands = (emb - 1) // 2
    fr = jnp.linspace(1e-4, bands - 1, bands, dtype=F32)
    wang = 2.0 * math.pi * fpos / S
    zfeat = jnp.concatenate([(fpos / max(S - 1, 1))[:, None], jnp.cos(wang[:, None] * fr),
                             -jnp.sin(wang[:, None] * fr)], axis=-1)
    epad = LANES
    zfeat = jnp.pad(zfeat, ((0, 0), (0, epad - emb)))

    assert B == 2, "the Hyena DFT carries the two batch rows as one complex signal"
    ffa = (_pack_gate_up(ffa_w_gate, ffa_w_up), _to_bf16(ffa_w_down))
    ffb = (_pack_gate_up(ffb_w_gate, ffb_w_up), _to_bf16(ffb_w_down))
    w_in_b, w_out_b = _to_bf16(w_in), _to_bf16(w_out)
    row = lambda v: v.reshape(1, -1)

    h = x.reshape(M, D)
    for l in range(depth):
        h = _ffn(h, *ffa, l, row(ln_gain[l, 0]), row(ln_bias[l, 0]), alpha)
        p3 = _inproj(h, w_in_b, l).reshape(B, S, -1)
        y_ret = _retention(p3, cos, sin, lg_tab, H)
        w1p = jnp.pad(hy_filt_w1[l], ((0, epad - emb), (0, 0)))
        filt_params = (w1p, row(hy_filt_b1[l]), row(hy_filt_freq1[l]), hy_filt_w2[l],
                       row(hy_filt_b2[l]), row(hy_filt_freq2[l]), hy_filt_w3[l], row(hy_decay[l]))
        y_hy = _hyena(p3, col_hy, Hy, tabs, hy_conv_w[l], hy_conv_b[l], filt_params, hy_bias[l], zfeat)
        y_lru = _lru(p3, col_lg, col_lx, Lw, lru_conv_w[l], lru_conv_b[l], lru_wa[l], lru_ba[l],
                     lru_wx[l], lru_bx[l], lru_lambda[l])
        yb = _mixnorm(y_ret.reshape(M, R), y_hy, y_lru.reshape(M, Lw),
                      row(mix_norm_gain[l]))
        h = _outproj(yb, w_out_b, l, h, row(ln_gain[l, 1]), row(ln_bias[l, 1]), alpha)
        h = _ffn(h, *ffb, l, row(ln_gain[l, 2]), row(ln_bias[l, 2]), alpha)
    return h.reshape(B, S, D)
```

```python
import functools
import math

import numpy as np
import jax
import jax.numpy as jnp
from jax import lax
from jax.experimental import pallas as pl
from jax.experimental.pallas import tpu as pltpu

F32 = jnp.float32
BF16 = jnp.bfloat16

LANES = 128
RET_HEAD_DIM = 256
LRU_BLOCK_DIM = 128
LRU_C = 8.0
VMEM_CAP = 60 * 1024 * 1024
LN_ROWS = 16


def _params(semantics, vmem_bytes):
    return pltpu.CompilerParams(
        dimension_semantics=semantics,
        vmem_limit_bytes=int(min(VMEM_CAP, max(vmem_bytes, 16 * 1024 * 1024))))


def _layer_norm_rows(y, g, b, eps=1e-5):
    mu = jnp.mean(y, axis=-1, keepdims=True)
    d = y - mu
    var = jnp.mean(d * d, axis=-1, keepdims=True)
    return d * lax.rsqrt(var + eps) * g + b


def _silu(x):
    return x * jax.nn.sigmoid(x)


def _ffn_body(x_ref, wgu_a, wd_a, g_ref, b_ref, o_ref, xb_ref, *, alpha, rows, slab):
    j = pl.program_id(1)
    nj = pl.num_programs(1)
    tm = x_ref.shape[0]
    tf = wd_a.shape[0]
    D = o_ref.shape[1]

    @pl.when(j == 0)
    def _():
        xb_ref[...] = x_ref[...].astype(BF16)
        o_ref[...] = jnp.zeros_like(o_ref)

    def hidden(wgu_ref):
        gu = jnp.dot(xb_ref[...], wgu_ref[...], preferred_element_type=F32)
        return (_silu(gu[:, :tf]) * gu[:, tf:]).astype(BF16)

    def accumulate(pairs):
        hs = [(hidden(wgu_ref), wd_ref) for wgu_ref, wd_ref in pairs]
        for c0 in range(0, D, slab):
            part = None
            for h, wd_ref in hs:
                p = jnp.dot(h, wd_ref[:, c0:c0 + slab], preferred_element_type=F32)
                part = p if part is None else part + p
            o_ref[:, c0:c0 + slab] += part

    accumulate([(wgu_a, wd_a)])

    @pl.when(j == nj - 1)
    def _():
        g = g_ref[...]
        b = b_ref[...]

        def body(r, carry):
            sl = pl.ds(pl.multiple_of(r * rows, rows), rows)
            y = alpha * x_ref[sl, :] + 0.5 * o_ref[sl, :]
            o_ref[sl, :] = _layer_norm_rows(y, g, b)
            return carry

        lax.fori_loop(0, tm // rows, body, 0, unroll=2)


def _cast_body(x_ref, o_ref):
    o_ref[...] = x_ref[...].astype(BF16)


def _to_bf16(w):
    L, R, C = w.shape
    tr = 256
    while tr > 8 and (R % tr or tr * C * 4 > (12 << 20)):
        tr //= 2
    spec = pl.BlockSpec((None, tr, C), lambda l, i: (l, i, 0))
    return pl.pallas_call(
        _cast_body,
        grid=(L, R // tr),
        in_specs=[spec],
        out_specs=spec,
        out_shape=jax.ShapeDtypeStruct(w.shape, BF16),
        compiler_params=_params(("parallel", "parallel"), 3 * tr * C * 4 + (8 << 20)),
        name="to_bf16",
    )(w)


FFN_TF = 256


def _pack_gate_up_body(g_ref, u_ref, o_ref, *, tf):
    for c in range(g_ref.shape[1] // tf):
        o_ref[c, :, :tf] = g_ref[:, c * tf:(c + 1) * tf].astype(BF16)
        o_ref[c, :, tf:] = u_ref[:, c * tf:(c + 1) * tf].astype(BF16)


def _pack_gate_up(wg, wu):
    L, D, F = wg.shape
    tf = FFN_TF if F % FFN_TF == 0 else F
    tr = min(128, D)
    spec = pl.BlockSpec((None, tr, F), lambda l, i: (l, i, 0))
    return pl.pallas_call(
        functools.partial(_pack_gate_up_body, tf=tf),
        grid=(L, D // tr),
        in_specs=[spec, spec],
        out_specs=pl.BlockSpec((None, F // tf, tr, 2 * tf), lambda l, i: (l, 0, i, 0)),
        out_shape=jax.ShapeDtypeStruct((L, F // tf, D, 2 * tf), BF16),
        compiler_params=_params(("parallel", "parallel"), 6 * tr * F * 4 + (8 << 20)),
        name="pack_gate_up",
    )(wg, wu)


def _ffn(x, wgu, wd, layer, g, b, alpha):
    M, D = x.shape
    F = wd.shape[1]
    tm = min(1024, M)
    tf = FFN_TF if F % FFN_TF == 0 else F
    nf = F // tf
    rows = min(LN_ROWS, tm)
    slab = 512 if D % 512 == 0 else D
    once = pl.Buffered(1)
    vmem = (tm * D * 4 + tm * D * 2 + tm * D * 4 + 2 * 3 * D * tf * 2 + 6 * tm * tf * 4
            + 2 * tm * slab * 4)
    return pl.pallas_call(
        functools.partial(_ffn_body, alpha=alpha, rows=rows, slab=slab),
        grid=(M // tm, nf),
        in_specs=[
            pl.BlockSpec((tm, D), lambda i, j: (i, 0), pipeline_mode=once),
            pl.BlockSpec((None, None, D, 2 * tf), lambda i, j: (layer, j, 0, 0)),
            pl.BlockSpec((None, tf, D), lambda i, j: (layer, j, 0)),
            pl.BlockSpec((1, D), lambda i, j: (0, 0)),
            pl.BlockSpec((1, D), lambda i, j: (0, 0)),
        ],
        out_specs=pl.BlockSpec((tm, D), lambda i, j: (i, 0), pipeline_mode=once),
        out_shape=jax.ShapeDtypeStruct((M, D), F32),
        scratch_shapes=[pltpu.VMEM((tm, D), BF16)],
        compiler_params=_params(("parallel", "arbitrary"), vmem + (4 << 20)),
        name="ffn",
    )(x, wgu, wd, g, b)


def _inproj_body(x_ref, w_ref, o_ref, xb_ref):
    @pl.when(pl.program_id(1) == 0)
    def _():
        xb_ref[...] = x_ref[...].astype(BF16)

    o_ref[...] = jnp.dot(xb_ref[...], w_ref[...], preferred_element_type=F32)


def _inproj(x, w, layer):
    M, K = x.shape
    N = w.shape[2]
    tm = min(1024, M)
    tn = 1024 if N % 1024 == 0 else (512 if N % 512 == 0 else 128)
    vmem = tm * K * 4 + tm * K * 2 + 2 * K * tn * 2 + 3 * tm * tn * 4
    return pl.pallas_call(
        _inproj_body,
        grid=(M // tm, N // tn),
        in_specs=[
            pl.BlockSpec((tm, K), lambda i, j: (i, 0), pipeline_mode=pl.Buffered(1)),
            pl.BlockSpec((None, K, tn), lambda i, j: (layer, 0, j)),
        ],
        out_specs=pl.BlockSpec((tm, tn), lambda i, j: (i, j)),
        out_shape=jax.ShapeDtypeStruct((M, N), F32),
        scratch_shapes=[pltpu.VMEM((tm, K), BF16)],
        compiler_params=_params(("parallel", "arbitrary"), vmem + (4 << 20)),
        name="inproj",
    )(x, w)


def _outproj_body(y_ref, w_ref, x_ref, g_ref, b_ref, o_ref, *, alpha, rows, nj, slab):
    j = pl.program_id(1)
    tm, D = o_ref.shape
    rc = x_ref.shape[0]

    @pl.when(j == 0)
    def _():
        o_ref[...] = jnp.zeros_like(o_ref)

    yk = y_ref[...]
    for c0 in range(0, D, slab):
        o_ref[:, c0:c0 + slab] += jnp.dot(yk, w_ref[:, c0:c0 + slab], preferred_element_type=F32)
    sl = pl.ds(pl.multiple_of(j * rc, rc), rc)
    o_ref[sl, :] += alpha * x_ref[...]

    @pl.when(j == nj - 1)
    def _():
        g = g_ref[...]
        b = b_ref[...]

        def body(r, carry):
            rs = pl.ds(pl.multiple_of(r * rows, rows), rows)
            o_ref[rs, :] = _layer_norm_rows(o_ref[rs, :], g, b)
            return carry

        lax.fori_loop(0, tm // rows, body, 0, unroll=2)


def _outproj(y, w, layer, x, g, b, alpha):
    M, K = y.shape
    D = w.shape[2]
    tm = min(512, M)
    tk = 1024 if K % 1024 == 0 else K
    nj = K // tk
    rc = tm // nj
    rows = min(LN_ROWS, tm)
    slab = 1024 if D % 1024 == 0 else D
    vmem = 2 * tm * tk * 2 + 2 * tk * D * 2 + 2 * rc * D * 4 + 2 * tm * D * 4 + 2 * tm * slab * 4
    return pl.pallas_call(
        functools.partial(_outproj_body, alpha=alpha, rows=rows, nj=nj, slab=slab),
        grid=(M // tm, nj),
        in_specs=[
            pl.BlockSpec((tm, tk), lambda i, j: (i, j)),
            pl.BlockSpec((None, tk, D), lambda i, j: (layer, j, 0)),
            pl.BlockSpec((rc, D), lambda i, j: (i * nj + j, 0)),
            pl.BlockSpec((1, D), lambda i, j: (0, 0)),
            pl.BlockSpec((1, D), lambda i, j: (0, 0)),
        ],
        out_specs=pl.BlockSpec((tm, D), lambda i, j: (i, 0)),
        out_shape=jax.ShapeDtypeStruct((M, D), F32),
        compiler_params=_params(("parallel", "arbitrary"), vmem + (6 << 20)),
        name="outproj",
    )(y, w, x, g, b)


def _rotary(x, cos, sin):
    half = x.shape[-1] // 2
    x1 = x[:, :half]
    x2 = x[:, half:]
    return jnp.concatenate([x1 * cos - x2 * sin, x1 * sin + x2 * cos], axis=-1)


def _ret_tables(cos_ref, sin_ref, chunk, C):
    rows = pl.ds(pl.multiple_of(chunk * C, C), C)
    return cos_ref[rows, :], sin_ref[rows, :]


def _row_index(shape):
    return lax.broadcasted_iota(jnp.int32, shape, 0).astype(F32)


def _ret_fwd_body(q_ref, k_ref, v_ref, cos_ref, sin_ref, lg_ref, y_ref,
                  state_ref, qd_ref, kd_ref, intra_ref):
    C = q_ref.shape[0]
    dk = RET_HEAD_DIM
    heads = lg_ref.shape[0]

    @pl.when(pl.program_id(2) == 0)
    def _():
        state_ref[...] = jnp.zeros_like(state_ref)
        row = _row_index((C, dk))
        ii = lax.broadcasted_iota(jnp.int32, (C, C), 0)
        jj = lax.broadcasted_iota(jnp.int32, (C, C), 1)
        dist = jnp.abs(ii - jj).astype(F32)
        for t in range(heads):
            lg = lg_ref[t, 0:1, :dk]
            qd_ref[t] = jnp.exp(lg * (row + 1.0))
            kd_ref[t] = jnp.exp(lg * (C - 1.0 - row))
            intra_ref[t] = jnp.exp(lg_ref[t, 0:1, :C] * dist)

    cos, sin = _ret_tables(cos_ref, sin_ref, pl.program_id(2), C)
    for t in range(heads):
        cols = slice(t * dk, (t + 1) * dk)
        lg = lg_ref[t, 0:1, :dk]
        q = _rotary(q_ref[:, cols], cos, sin)
        k = _rotary(k_ref[:, cols], cos, sin) * (dk ** -0.5)
        vb = v_ref[:, cols].astype(BF16)
        q_in = (q * qd_ref[t]).astype(BF16)
        k_out = (k * kd_ref[t]).astype(BF16)
        s = lax.dot_general(q.astype(BF16), k.astype(BF16), (((1,), (1,)), ((), ())),
                            preferred_element_type=F32) * intra_ref[t]
        state = state_ref[t]
        y = jnp.dot(s.astype(BF16), vb, preferred_element_type=F32)
        y_ref[:, cols] = y + jnp.dot(q_in, state.astype(BF16), preferred_element_type=F32)
        kv = lax.dot_general(k_out, vb, (((0,), (0,)), ((), ())), preferred_element_type=F32)
        state_ref[t] = state * jnp.exp(lg * float(C)) + kv


def _ret_bwd_body(q_ref, k_ref, v_ref, g_ref, cos_ref, sin_ref, lg_ref, yf_ref, y_ref,
                  state_ref, qd_ref, kd_ref):
    C = q_ref.shape[0]
    dk = RET_HEAD_DIM
    heads = lg_ref.shape[0]

    @pl.when(pl.program_id(2) == 0)
    def _():
        state_ref[...] = jnp.zeros_like(state_ref)
        row = _row_index((C, dk))
        for t in range(heads):
            lg = lg_ref[t, 0:1, :dk]
            qd_ref[t] = jnp.exp(lg * (C - row))
            kd_ref[t] = jnp.exp(lg * row)

    cos, sin = _ret_tables(cos_ref, sin_ref, pl.num_programs(2) - 1 - pl.program_id(2), C)
    for t in range(heads):
        cols = slice(t * dk, (t + 1) * dk)
        lg = lg_ref[t, 0:1, :dk]
        q = _rotary(q_ref[:, cols], cos, sin)
        k = _rotary(k_ref[:, cols], cos, sin) * (dk ** -0.5)
        vb = v_ref[:, cols].astype(BF16)
        q_in = (q * qd_ref[t]).astype(BF16)
        k_out = (k * kd_ref[t]).astype(BF16)
        state = state_ref[t]
        y = yf_ref[:, cols] + jnp.dot(q_in, state.astype(BF16), preferred_element_type=F32)
        kv = lax.dot_general(k_out, vb, (((0,), (0,)), ((), ())), preferred_element_type=F32)
        state_ref[t] = state * jnp.exp(lg * float(C)) + kv
        mu = jnp.mean(y, axis=-1, keepdims=True)
        d = y - mu
        var = jnp.mean(d * d, axis=-1, keepdims=True)
        y_ref[:, cols] = _silu(g_ref[:, cols]) * (d * lax.rsqrt(var + 1e-5))


def _retention(p3, cos, sin, lg_tab, H):
    B, S, _ = p3.shape
    dk = RET_HEAD_DIM
    C = min(512, S)
    nc = S // C
    W = lg_tab.shape[-1]
    hp = next(n for n in (4, 2, 1) if H % n == 0)
    G = H // hp
    wb = hp * dk
    blk = lambda off, rev: pl.BlockSpec(
        (None, C, wb),
        (lambda b, g, c: (b, nc - 1 - c, off // hp + g)) if rev else (lambda b, g, c: (b, c, off // hp + g)))
    tab = pl.BlockSpec((S, dk // 2), lambda b, g, c: (0, 0))
    lg_spec = pl.BlockSpec((hp, 8, W), lambda b, g, c: (g, 0, 0))
    vmem = 40 << 20
    tables = [pltpu.VMEM((hp, dk, dk), F32), pltpu.VMEM((hp, C, dk), F32), pltpu.VMEM((hp, C, dk), F32)]
    y_f = pl.pallas_call(
        _ret_fwd_body,
        grid=(B, G, nc),
        in_specs=[blk(0, False), blk(H, False), blk(2 * H, False), tab, tab, lg_spec],
        out_specs=blk(0, False),
        out_shape=jax.ShapeDtypeStruct((B, S, H * dk), F32),
        scratch_shapes=tables + [pltpu.VMEM((hp, C, C), F32)],
        compiler_params=_params(("parallel", "parallel", "arbitrary"), vmem),
        name="ret_fwd",
    )(p3, p3, p3, cos, sin, lg_tab)
    return pl.pallas_call(
        _ret_bwd_body,
        grid=(B, G, nc),
        in_specs=[blk(0, True), blk(H, True), blk(2 * H, True), blk(3 * H, True),
                  tab, tab, lg_spec, blk(0, True)],
        out_specs=blk(0, True),
        out_shape=jax.ShapeDtypeStruct((B, S, H * dk), F32),
        scratch_shapes=tables,
        compiler_params=_params(("parallel", "parallel", "arbitrary"), vmem),
        name="ret_bwd",
    )(p3, p3, p3, p3, cos, sin, lg_tab, y_f)


HALO = 8


def _fill_haloed(xs_ref, x_ref, S):
    zeros = jnp.zeros((HALO, xs_ref.shape[1]), F32)
    xs_ref[0:HALO, :] = zeros
    xs_ref[S + HALO:S + 2 * HALO, :] = zeros
    xs_ref[HALO:S + HALO, :] = x_ref[...]


def _conv_rows(xs_ref, r0, R, taps, left, bias):
    ext = xs_ref[pl.ds(r0, R + 2 * HALO), :]
    n = R + 2 * HALO
    acc = None
    for j, w in enumerate(taps):
        d = j - left
        sh = ext if d == 0 else pltpu.roll(ext, (-d) % n, axis=0)
        term = w * sh[HALO:HALO + R, :]
        acc = term if acc is None else acc + term
    return acc + bias


FFT_N2 = 128
TN2 = 32
NT = FFT_N2 // TN2
PITCH = TN2 + 4
STAGE_CB = LANES


def _store_tiled_pair(o_ref, lead, blocks):
    pad = jnp.zeros((PITCH - TN2, blocks[0].shape[1]), F32)
    for h in range(NT):
        for q, block in enumerate(blocks):
            o_ref[(h, *lead, slice(q * PITCH, q * PITCH + TN2))] = block[h * TN2:(h + 1) * TN2, :]
            o_ref[(h, *lead, slice(q * PITCH + TN2, (q + 1) * PITCH))] = pad


def _hy_conv_body(u_ref, w_ref, b_ref, o_ref, xs_ref, *, S, R, K):
    _fill_haloed(xs_ref, u_ref, S)
    taps = [w_ref[j:j + 1, :] for j in range(K)]
    bias = b_ref[...]

    def body(i, carry):
        r0 = pl.multiple_of(i * R, R)
        res = _conv_rows(xs_ref, r0, R, taps, (K - 1) // 2, bias)
        _store_tiled_pair(o_ref, (i,), [res[:FFT_N2], res[FFT_N2:]])
        return carry

    lax.fori_loop(0, S // R, body, 0)


def _hy_conv(p3, col0, width, w, b):
    B, S, _ = p3.shape
    K = w.shape[0]
    cb = LANES
    nb = width // cb
    C = width // 3
    R = 2 * FFT_N2
    npair = S // R
    wp = jnp.zeros((8, width), F32).at[:K].set(w)
    off = col0 // cb
    per = C // cb
    return pl.pallas_call(
        functools.partial(_hy_conv_body, S=S, R=R, K=K),
        grid=(B, nb),
        in_specs=[
            pl.BlockSpec((None, S, cb), lambda b_, j: (b_, 0, off + j)),
            pl.BlockSpec((8, cb), lambda b_, j: (0, j)),
            pl.BlockSpec((1, cb), lambda b_, j: (0, j)),
        ],
        out_specs=pl.BlockSpec((None, NT, npair, 2 * PITCH, cb),
                               lambda b_, j: ((j // per) * B + b_, 0, 0, 0, j % per)),
        out_shape=jax.ShapeDtypeStruct((3 * B, NT, npair, 2 * PITCH, C), F32),
        scratch_shapes=[pltpu.VMEM((S + 2 * HALO, cb), F32)],
        compiler_params=_params(("parallel", "parallel"), 7 * S * cb * 4 + (8 << 20)),
        name="hy_conv",
    )(p3, wp, b.reshape(1, width)).reshape(3 * B, NT, npair * 2 * PITCH, C)


def _dot_split(a, b):
    a_hi = a.astype(BF16).astype(F32)
    b_hi = b.astype(BF16).astype(F32)
    lhs = jnp.concatenate([a_hi, a_hi, a - a_hi], axis=1).astype(BF16)
    rhs = jnp.concatenate([b_hi, b - b_hi, b_hi], axis=0).astype(BF16)
    return jnp.dot(lhs, rhs, preferred_element_type=F32)


def _hy_filter_body(z_ref, w1_ref, b1_ref, f1_ref, w2_ref, b2_ref, f2_ref, w3_ref, dec_ref,
                    o_ref, sum_ref, *, L, tr):
    i = pl.program_id(0)
    z = z_ref[...]
    h = jnp.sin(f1_ref[...] * (_dot_split(z, w1_ref[...]) + b1_ref[...]))
    h = jnp.sin(f2_ref[...] * (_dot_split(h, w2_ref[...]) + b2_ref[...]))
    h = _dot_split(h, w3_ref[...])
    t = z[:, 0:1]
    h = h * jnp.exp(-t * jnp.abs(dec_ref[...]))
    m = i * tr + lax.broadcasted_iota(jnp.int32, h.shape, 0)
    filt = jnp.where(m == L, 0.0, h)
    _store_tiled_pair(o_ref, (), [filt[:FFT_N2], filt[FFT_N2:]])

    @pl.when(i == 0)
    def _():
        sum_ref[...] = jnp.zeros_like(sum_ref)

    sum_ref[...] += jnp.sum(jnp.abs(filt), axis=0, keepdims=True)


def _hy_filter(zfeat, w1, b1, f1, w2, b2, f2, w3, dec, L):
    N, E = zfeat.shape
    hid = w1.shape[1]
    C4 = w3.shape[1]
    tr = 2 * FFT_N2
    nt_half = L // tr
    full = lambda a: pl.BlockSpec(a.shape, lambda i: (0,) * a.ndim)
    side = lambda a: pl.BlockSpec((a.shape[0], C4 // 2), lambda i: (0, i // nt_half))
    args = (w1, b1, f1, w2, b2, f2, w3, dec)
    return pl.pallas_call(
        functools.partial(_hy_filter_body, L=L, tr=tr),
        grid=(N // tr,),
        in_specs=[pl.BlockSpec((tr, E), lambda i: (i, 0))] + [full(a) for a in args[:6]]
        + [side(w3), side(dec)],
        out_specs=[pl.BlockSpec((NT, None, 2 * PITCH, C4 // 2), lambda i: (0, i, 0, 0)),
                   pl.BlockSpec((1, C4 // 2), lambda i: (0, 0))],
        out_shape=[jax.ShapeDtypeStruct((NT, N // tr, 2 * PITCH, C4 // 2), F32),
                   jax.ShapeDtypeStruct((1, C4 // 2), F32)],
        compiler_params=_params(("arbitrary",), 32 << 20),
        name="hy_filter",
    )(zfeat, *args)


def _stage_a_body(w_ref, x_ref, o_ref, *, nparts, rows):
    mo = w_ref.shape[0]
    w = w_ref[...]

    def body(r, carry):
        parts = [x_ref[p, pl.ds(r, rows, stride=PITCH), :] for p in range(nparts)]
        x = parts[0] if nparts == 1 else jnp.concatenate(parts, axis=0)
        o_ref[pl.ds(r, mo, stride=PITCH), :] = jnp.dot(w, x.astype(BF16), preferred_element_type=F32)
        return carry

    lax.fori_loop(0, TN2, body, 0, unroll=8)
    pad = jnp.zeros((mo, o_ref.shape[1]), F32)
    for r in range(TN2, PITCH):
        o_ref[pl.ds(r, mo, stride=PITCH), :] = pad


def _stage_a(wmat, x, group, nparts):
    mo, K = wmat.shape
    _, _, T, C = x.shape
    rows = T // PITCH
    assert K == nparts * rows
    cb = min(STAGE_CB, C)
    vmem = 2 * nparts * T * cb * 4 + 2 * mo * PITCH * cb * 4 + (12 << 20)
    return pl.pallas_call(
        functools.partial(_stage_a_body, nparts=nparts, rows=rows),
        grid=(C // cb, NT),
        in_specs=[pl.BlockSpec((mo, K), lambda j, h: (0, 0)),
                  pl.BlockSpec((nparts, None, T, cb), lambda j, h: (group, h, 0, j))],
        out_specs=pl.BlockSpec((None, mo * PITCH, cb), lambda j, h: (h, 0, j)),
        out_shape=jax.ShapeDtypeStruct((NT, mo * PITCH, C), F32),
        compiler_params=_params(("parallel", "parallel"), vmem),
        name="hy_stage_a",
    )(wmat, x)


def _hy_freq_body(g_ref, gi_ref, x_ref, f_ref, s_ref, o_ref, *, kb):
    n2 = FFT_N2
    inv = 1.0 / (s_ref[...] + 1e-6)

    def gather(ref, t):
        lo = (t % 2) * PITCH
        return jnp.concatenate([ref[h, ri, t // 2, lo:lo + TN2, :] for ri in range(2) for h in range(NT)],
                               axis=0)

    pad = jnp.zeros((PITCH - TN2, o_ref.shape[-1]), F32)
    for t in range(kb):
        g = g_ref[t]
        y = jnp.dot(g, gather(x_ref, t).astype(BF16), preferred_element_type=F32)
        h = jnp.dot(g, gather(f_ref, t).astype(BF16), preferred_element_type=F32) * inv
        yr, yi = y[:n2], y[n2:]
        hr, hi = h[:n2], h[n2:]
        z = jnp.concatenate([yr * hr - yi * hi, yr * hi + yi * hr], axis=0).astype(BF16)
        e = jnp.dot(gi_ref[t], z, preferred_element_type=F32)
        lo = (t % 2) * PITCH
        for ri in range(2):
            for hh in range(NT):
                o_ref[hh, ri, t // 2, lo:lo + TN2, :] = e[ri * n2 + hh * TN2:ri * n2 + (hh + 1) * TN2]
                o_ref[hh, ri, t // 2, lo + TN2:lo + PITCH, :] = pad


def _hy_freq(gmat, gimat, xa, fa, asum, order, C):
    N1 = 2 * xa.shape[2]
    n2 = FFT_N2
    kb = min(4, N1)
    blk = pl.BlockSpec((NT, 2, kb // 2, 2 * PITCH, C), lambda i: (0, 0, i, 0, 0))
    gspec = pl.BlockSpec((kb, 2 * n2, 2 * n2), lambda i: (i, 0, 0))
    return pl.pallas_call(
        functools.partial(_hy_freq_body, kb=kb),
        grid=(N1 // kb,),
        in_specs=[gspec, gspec, blk,
                  pl.BlockSpec((NT, 2, kb // 2, 2 * PITCH, C), lambda i: (0, 0, i, 0, order)),
                  pl.BlockSpec((1, C), lambda i: (0, order))],
        out_specs=blk,
        out_shape=jax.ShapeDtypeStruct(xa.shape, F32),
        compiler_params=_params(("parallel",), 6 * 2 * kb * n2 * C * 4 + 24 * n2 * C * 4 + (8 << 20)),
        name="hy_freq",
    )(gmat, gimat, xa, fa, asum)


def _stage_d_body(w_ref, e_ref, z_ref, x_ref, b_ref, o_ref, *, B, nh):
    w = w_ref[...]
    bias = b_ref[...]
    mi = w.shape[1]

    def body(r, carry):
        e = e_ref[pl.ds(r, mi, stride=PITCH), :].astype(BF16)
        conv = jnp.dot(w, e, preferred_element_type=F32)
        rows = pl.ds(r, nh, stride=PITCH)
        for b in range(B):
            o_ref[b, rows, :] = x_ref[b, rows, :] * (conv[b * nh:(b + 1) * nh] + bias * z_ref[b, rows, :])
        return carry

    lax.fori_loop(0, TN2, body, 0, unroll=8)
    pad = jnp.zeros((nh, o_ref.shape[-1]), F32)
    for r in range(TN2, PITCH):
        for b in range(B):
            o_ref[b, pl.ds(r, nh, stride=PITCH), :] = pad


def _stage_d(w4, e, z, z_group, gate, gate_group, bias, B):
    mo, mi = w4.shape
    nh = mo // B
    T, C = z.shape[2], z.shape[3]
    cb = min(STAGE_CB, C)
    nat = lambda grp: pl.BlockSpec((B, None, T, cb), lambda j, h: (grp, h, 0, j))
    vmem = 2 * mi * PITCH * cb * 4 + 6 * B * T * cb * 4 + (12 << 20)
    return pl.pallas_call(
        functools.partial(_stage_d_body, B=B, nh=nh),
        grid=(C // cb, NT),
        in_specs=[pl.BlockSpec((mo, mi), lambda j, h: (0, 0)),
                  pl.BlockSpec((None, mi * PITCH, cb), lambda j, h: (h, 0, j)),
                  nat(z_group), nat(gate_group),
                  pl.BlockSpec((1, cb), lambda j, h: (0, j))],
        out_specs=nat(0),
        out_shape=jax.ShapeDtypeStruct((B, NT, T, C), F32),
        compiler_params=_params(("parallel", "parallel"), vmem),
        name="hy_stage_d",
    )(w4, e, z, gate, bias)


def _dft_tables(S):
    N = 2 * S
    n2 = FFT_N2
    N1 = N // n2
    nh = S // n2
    k1 = np.arange(N1)
    f1 = np.exp(-2j * np.pi * np.outer(k1, np.arange(N1)) / N1)
    fa = f1[:, :nh]
    wa = np.block([[fa.real, -fa.imag], [fa.imag, fa.real]])
    wa_real = np.concatenate([f1.real, f1.imag], axis=0)
    a2 = np.arange(n2)
    base = np.exp(-2j * np.pi * np.outer(a2, a2) / n2)
    tw = np.exp(-2j * np.pi * np.outer(k1, a2) / N)
    g = base[None, :, :] * tw[:, None, :]
    gi = np.conj(np.transpose(g, (0, 2, 1)))
    blockify = lambda c: np.concatenate(
        [np.concatenate([c.real, -c.imag], axis=-1), np.concatenate([c.imag, c.real], axis=-1)], axis=-2)
    f4 = np.exp(2j * np.pi * np.outer(np.arange(nh), k1) / N1) / N
    wd = np.block([[f4.real, -f4.imag], [f4.imag, f4.real]])
    cast = lambda a: jnp.asarray(a, dtype=F32).astype(BF16)
    return dict(N1=N1, nh=nh, wa=cast(wa), wa_real=cast(wa_real), g=cast(blockify(g)),
                gi=cast(blockify(gi)), wd=cast(wd))


def _hyena(p3, col0, C, tabs, conv_w, conv_b, filt_params, hy_bias, zfeat):
    B, S, _ = p3.shape
    N1 = tabs["N1"]
    u = _hy_conv(p3, col0, 3 * C, conv_w, conv_b)
    filt, asum = _hy_filter(zfeat, *filt_params, S)
    filt = filt.reshape(1, NT, N1 * PITCH, 2 * C)
    fa = _stage_a(tabs["wa_real"], filt, 0, 1).reshape(NT, 2, N1 // 2, 2 * PITCH, 2 * C)
    z = u
    for order in range(2):
        xa = _stage_a(tabs["wa"], z, 0, B)
        e = _hy_freq(tabs["g"], tabs["gi"], xa.reshape(NT, 2, N1 // 2, 2 * PITCH, C), fa, asum, order, C)
        z = _stage_d(tabs["wd"], e.reshape(NT, 2 * N1 * PITCH, C), z, 0, u, 1 + order,
                     hy_bias[order].reshape(1, C), B)
    return z


def _scan_rows(a, b, reverse):
    R = a.shape[0]
    row = lax.broadcasted_iota(jnp.int32, a.shape, 0)
    d = 1
    while d < R:
        if reverse:
            a_s = pltpu.roll(a, R - d, axis=0)
            b_s = pltpu.roll(b, R - d, axis=0)
            m = row < R - d
        else:
            a_s = pltpu.roll(a, d, axis=0)
            b_s = pltpu.roll(b, d, axis=0)
            m = row >= d
        b = jnp.where(m, a * b_s + b, b)
        a = jnp.where(m, a * a_s, a)
        d *= 2
    return a, b


def _gelu_tanh(x):
    return 0.5 * x * (1.0 + jnp.tanh(0.7978845608028654 * (x + 0.044715 * x * x * x)))


def _lru_body(g_ref, x_ref, cw_ref, cb_ref, w_ref, bias_ref, lam_ref, o_ref,
              xs_ref, hf_ref, ab_ref, bb_ref, *, S, R, K):
    bd = x_ref.shape[1]
    _fill_haloed(xs_ref, x_ref, S)
    taps = [cw_ref[j:j + 1, :] for j in range(K)]
    cbias = cb_ref[...]
    w = w_ref[...]
    bias = bias_ref[...]
    nl = -lam_ref[...]
    sp = jnp.maximum(nl, 0.0) + jnp.log1p(jnp.exp(-jnp.abs(nl)))
    sp0 = sp[0:1, :]
    sp1 = sp[1:2, :]
    nchunk = S // R

    def gates(r, i, spd, xr):
        log_a = -LRU_C * jax.nn.sigmoid(r) * spd
        a = jnp.exp(log_a)
        b = jnp.sqrt(1.0 - a * a) * (jax.nn.sigmoid(i) * xr)
        return a, b

    def fwd(i, hc):
        r0 = pl.multiple_of(i * R, R)
        xr = _conv_rows(xs_ref, r0, R, taps, (K - 1) // 2, cbias)
        proj = jnp.dot(xr.astype(BF16), w, preferred_element_type=F32) + bias
        a0, b0 = gates(proj[:, 0:bd], proj[:, bd:2 * bd], sp0, xr)
        a1, b1 = gates(proj[:, 2 * bd:3 * bd], proj[:, 3 * bd:4 * bd], sp1, xr)
        ab_ref[pl.ds(r0, R), :] = a1
        bb_ref[pl.ds(r0, R), :] = b1
        ac, hl = _scan_rows(a0, b0, reverse=False)
        h = ac * hc + hl
        hf_ref[pl.ds(r0, R), :] = h
        return h[R - 1:R, :]

    lax.fori_loop(0, nchunk, fwd, jnp.zeros((1, bd), F32))

    def bwd(t, hc):
        r0 = pl.multiple_of((nchunk - 1 - t) * R, R)
        sl = pl.ds(r0, R)
        ac, hl = _scan_rows(ab_ref[sl, :], bb_ref[sl, :], reverse=True)
        h = ac * hc + hl
        o_ref[sl, :] = _gelu_tanh(g_ref[sl, :]) * (hf_ref[sl, :] + h)
        return h[0:1, :]

    lax.fori_loop(0, nchunk, bwd, jnp.zeros((1, bd), F32))


def _lru(p3, col_g, col_x, width, conv_w, conv_b, wa, ba, wx, bx, lam):
    B, S, _ = p3.shape
    bd = LRU_BLOCK_DIM
    nb = width // bd
    K = conv_w.shape[0]
    R = min(256, S)
    cw = jnp.zeros((8, width), F32).at[:K].set(conv_w)
    wcat = jnp.concatenate([wa[0], wx[0], wa[1], wx[1]], axis=-1).astype(BF16)
    bsplit = lambda v: v.reshape(nb, 1, bd)
    bcat = jnp.concatenate([bsplit(ba[0]), bsplit(bx[0]), bsplit(ba[1]), bsplit(bx[1])], axis=-1)
    lam8 = jnp.zeros((8, width), F32).at[:2].set(lam)
    og, ox = col_g // bd, col_x // bd
    return pl.pallas_call(
        functools.partial(_lru_body, S=S, R=R, K=K),
        grid=(B, nb),
        in_specs=[
            pl.BlockSpec((None, S, bd), lambda b, j: (b, 0, og + j)),
            pl.BlockSpec((None, S, bd), lambda b, j: (b, 0, ox + j)),
            pl.BlockSpec((8, bd), lambda b, j: (0, j)),
            pl.BlockSpec((1, bd), lambda b, j: (0, j)),
            pl.BlockSpec((None, bd, 4 * bd), lambda b, j: (j, 0, 0)),
            pl.BlockSpec((None, 1, 4 * bd), lambda b, j: (j, 0, 0)),
            pl.BlockSpec((8, bd), lambda b, j: (0, j)),
        ],
        out_specs=pl.BlockSpec((None, S, bd), lambda b, j: (b, 0, j)),
        out_shape=jax.ShapeDtypeStruct((B, S, width), F32),
        scratch_shapes=[pltpu.VMEM((S + 2 * HALO, bd), F32), pltpu.VMEM((S, bd), F32),
                        pltpu.VMEM((S, bd), F32), pltpu.VMEM((S, bd), F32)],
        compiler_params=_params(("parallel", "parallel"), 10 * S * bd * 4 + (12 << 20)),
        name="lru",
    )(p3, p3, cw, conv_b.reshape(1, width), wcat, bcat, lam8)


def _mixnorm_body(a_ref, b_ref, c_ref, g_ref, o_ref):
    off = 0
    for ref in (a_ref, b_ref, c_ref):
        if len(ref.shape) == 3:
            y = jnp.concatenate([ref[h, q * PITCH:q * PITCH + TN2, :] for q in range(2) for h in range(NT)],
                                axis=0)
        else:
            y = ref[...]
        w = y.shape[1]
        ms = jnp.mean(y * y, axis=-1, keepdims=True)
        o_ref[:, off:off + w] = (y * lax.rsqrt(ms + 1e-6) * g_ref[:, off:off + w]).astype(BF16)
        off += w


def _mixnorm(ya, yb_tiled, yc, gain):
    M = ya.shape[0]
    B, _, T, wb = yb_tiled.shape
    npair = T // (2 * PITCH)
    widths = (ya.shape[1], wb, yc.shape[1])
    D = sum(widths)
    tm = 2 * FFT_N2
    nat = lambda w: pl.BlockSpec((tm, w), lambda i: (i, 0))
    tiled = pl.BlockSpec((None, NT, None, 2 * PITCH, wb), lambda i: (i // npair, 0, i % npair, 0, 0))
    return pl.pallas_call(
        _mixnorm_body,
        grid=(M // tm,),
        in_specs=[nat(widths[0]), tiled, nat(widths[2]), pl.BlockSpec((1, D), lambda i: (0, 0))],
        out_specs=pl.BlockSpec((tm, D), lambda i: (i, 0)),
        out_shape=jax.ShapeDtypeStruct((M, D), BF16),
        compiler_params=_params(("parallel",), 32 << 20),
        name="mixnorm",
    )(ya, yb_tiled.reshape(B, NT, npair, 2 * PITCH, wb), yc, gain)


def kernel(x, ffa_w_gate, ffa_w_up, ffa_w_down, ffb_w_gate, ffb_w_up, ffb_w_down, ln_gain, ln_bias,
           w_in, w_out, mix_norm_gain, hy_conv_w, hy_conv_b, hy_filt_w1, hy_filt_b1, hy_filt_freq1,
           hy_filt_w2, hy_filt_b2, hy_filt_freq2, hy_filt_w3, hy_decay, hy_bias, lru_conv_w,
           lru_conv_b, lru_wa, lru_ba, lru_wx, lru_bx, lru_lambda):
    B, S, D = x.shape
    depth = w_in.shape[0]
    M = B * S
    alpha = (2 * depth) ** 0.25
    R = D // 2
    H = R // RET_HEAD_DIM
    Hy = D // 4
    Lw = D // 4
    col_hy = 4 * R
    col_lg = col_hy + 3 * Hy
    col_lx = col_lg + Lw

    pos = jnp.arange(S, dtype=F32)
    half = RET_HEAD_DIM // 2
    inv_freq = 1.0 / (10000.0 ** jnp.linspace(0.0, 1.0, half, dtype=F32))
    ang = pos[:, None] * inv_freq[None, :]
    cos, sin = jnp.cos(ang), jnp.sin(ang)
    log_g = jnp.log1p(-jnp.exp2(-5.0 - jnp.arange(H, dtype=F32)))
    lg_tab = jnp.broadcast_to(log_g[:, None, None], (H, 8, max(512, RET_HEAD_DIM)))
    tabs = _dft_tables(S)
    m = jnp.arange(2 * S)
    fpos = jnp.where(m < S, m, 2 * S - m).astype(F32)
    emb = hy_filt_w1.shape[1]
    bands = (emb - 1) // 2
    fr = jnp.linspace(1e-4, bands - 1, bands, dtype=F32)
    wang = 2.0 * math.pi * fpos / S
    zfeat = jnp.concatenate([(fpos / max(S - 1, 1))[:, None], jnp.cos(wang[:, None] * fr),
                             -jnp.sin(wang[:, None] * fr)], axis=-1)
    epad = LANES
    zfeat = jnp.pad(zfeat, ((0, 0), (0, epad - emb)))

    assert B == 2, "the Hyena DFT carries the two batch rows as one complex signal"
    ffa = (_pack_gate_up(ffa_w_gate, ffa_w_up), _to_bf16(ffa_w_down))
    ffb = (_pack_gate_up(ffb_w_gate, ffb_w_up), _to_bf16(ffb_w_down))
    w_in_b, w_out_b = _to_bf16(w_in), _to_bf16(w_out)
    row = lambda v: v.reshape(1, -1)

    h = x.reshape(M, D)
    for l in range(depth):
        h = _ffn(h, *ffa, l, row(ln_gain[l, 0]), row(ln_bias[l, 0]), alpha)
        p3 = _inproj(h, w_in_b, l).reshape(B, S, -1)
        y_ret = _retention(p3, cos, sin, lg_tab, H)
        w1p = jnp.pad(hy_filt_w1[l], ((0, epad - emb), (0, 0)))
        filt_params = (w1p, row(hy_filt_b1[l]), row(hy_filt_freq1[l]), hy_filt_w2[l],
                       row(hy_filt_b2[l]), row(hy_filt_freq2[l]), hy_filt_w3[l], row(hy_decay[l]))
        y_hy = _hyena(p3, col_hy, Hy, tabs, hy_conv_w[l], hy_conv_b[l], filt_params, hy_bias[l], zfeat)
        y_lru = _lru(p3, col_lg, col_lx, Lw, lru_conv_w[l], lru_conv_b[l], lru_wa[l], lru_ba[l],
                     lru_wx[l], lru_bx[l], lru_lambda[l])
        yb = _mixnorm(y_ret.reshape(M, R), y_hy, y_lru.reshape(M, Lw),
                      row(mix_norm_gain[l]))
        h = _outproj(yb, w_out_b, l, h, row(ln_gain[l, 1]), row(ln_bias[l, 1]), alpha)
        h = _ffn(h, *ffb, l, row(ln_gain[l, 2]), row(ln_bias[l, 2]), alpha)
    return h.reshape(B, S, D)
```
